```python
import math, functools
import jax, jax.numpy as jnp
from jax import lax
import numpy as np

D_MODEL = 4096
BATCH = 4
SEQ = 4096
DEPTH = 1

N_META = 16
Q_BLOCK = 128
FOX_HEADS = 16
FOX_HEAD_DIM = 128
DIFF_HEADS = 8
DIFF_HEAD_DIM = 128
D_FF = 11008
RMS_EPS = 1e-6
ALIBI_MAX_BIAS = 8.0

FOX_WIDTH = FOX_HEADS * FOX_HEAD_DIM
DIFF_QK_WIDTH = DIFF_HEADS * 2 * DIFF_HEAD_DIM
DIFF_V_WIDTH = DIFF_HEADS * 2 * DIFF_HEAD_DIM
IN_SPLITS = [FOX_WIDTH, 2 * FOX_WIDTH, 3 * FOX_WIDTH, 3 * FOX_WIDTH + FOX_HEADS,
             3 * FOX_WIDTH + FOX_HEADS + DIFF_QK_WIDTH,
             3 * FOX_WIDTH + FOX_HEADS + 2 * DIFF_QK_WIDTH,
             3 * FOX_WIDTH + FOX_HEADS + 2 * DIFF_QK_WIDTH + DIFF_V_WIDTH,
             3 * FOX_WIDTH + FOX_HEADS + 2 * DIFF_QK_WIDTH + DIFF_V_WIDTH + D_MODEL]
IN_COLS = IN_SPLITS[-1] + D_MODEL

kernel_name = "fox_diffattn_gated_macaron_block"


def rms_norm(x, g):
    xf = x.astype(jnp.float32)
    y = xf * lax.rsqrt(jnp.mean(xf * xf, axis=-1, keepdims=True) + RMS_EPS)
    return (y * g.astype(jnp.float32)).astype(x.dtype)


def swiglu(x, w_gate, w_up, w_down):
    return (jax.nn.silu(x @ w_gate) * (x @ w_up)) @ w_down


def sweep_query_blocks(block_fn, q_args):
    L = q_args[0].shape[1]
    n_blocks = (L - N_META) // Q_BLOCK
    meta_out = block_fn(tuple(a[:, :N_META] for a in q_args), jnp.arange(N_META, dtype=jnp.int32))

    def to_blocks(a):
        a = a[:, N_META:]
        a = a.reshape(a.shape[0], n_blocks, Q_BLOCK, *a.shape[2:])
        return jnp.moveaxis(a, 1, 0)

    t_real = N_META + jnp.arange(n_blocks * Q_BLOCK, dtype=jnp.int32).reshape(n_blocks, Q_BLOCK)
    real_out = lax.map(lambda xs: block_fn(xs[0], xs[1]),
                       (tuple(to_blocks(a) for a in q_args), t_real))
    real_out = jnp.moveaxis(real_out, 0, 1)
    real_out = real_out.reshape(real_out.shape[0], n_blocks * Q_BLOCK, *real_out.shape[3:])
    return jnp.concatenate([meta_out, real_out], axis=1)


def fox_block(q_args, t_q, k, v, c_k):
    q, c_q = q_args
    k_pos = jnp.arange(k.shape[1], dtype=jnp.int32)
    s = jnp.einsum('bqhd,bkhd->bhqk', q, k, preferred_element_type=jnp.float32) * (FOX_HEAD_DIM ** -0.5)
    s = s + jnp.swapaxes(c_q, 1, 2)[..., None] - jnp.swapaxes(c_k, 1, 2)[:, :, None, :]
    causal = t_q[:, None] >= k_pos[None, :]
    s = jnp.where(causal, s, -jnp.inf)
    p = jax.nn.softmax(s, axis=-1).astype(v.dtype)
    return jnp.einsum('bhqk,bkhd->bqhd', p, v)


def diff_block(q_args, t_q, k, v, lam, slopes):
    (q,) = q_args
    k_pos = jnp.arange(k.shape[1], dtype=jnp.int32)
    s = jnp.einsum('bqhcd,bkhcd->bhcqk', q, k, preferred_element_type=jnp.float32) * (DIFF_HEAD_DIM ** -0.5)
    dist = (t_q[:, None] - k_pos[None, :]).astype(jnp.float32)
    s = s - slopes[None, :, None, None, None] * dist[None, None, None]
    causal = t_q[:, None] >= k_pos[None, :]
    s = jnp.where(causal, s, -jnp.inf)
    p = jax.nn.softmax(s, axis=-1)
    a = p[:, :, 0] - lam * p[:, :, 1]
    return jnp.einsum('bhqk,bkhd->bqhd', a.astype(v.dtype), v)


def hybrid_mixer(u, w_in, b_forget, lambda_q1, lambda_k1, lambda_q2, lambda_k2, diff_subln_g,
                 w_o_fox, w_o_diff, w_out, lambda_init):
    B, L, _ = u.shape
    proj = u @ w_in
    fq, fk, fv, f_logit, dq, dk, dv, g_fox, g_diff = jnp.split(proj, IN_SPLITS, axis=-1)

    fq = fq.reshape(B, L, FOX_HEADS, FOX_HEAD_DIM)
    fk = fk.reshape(B, L, FOX_HEADS, FOX_HEAD_DIM)
    fv = fv.reshape(B, L, FOX_HEADS, FOX_HEAD_DIM)
    log_f = jax.nn.log_sigmoid(f_logit.astype(jnp.float32) + b_forget.astype(jnp.float32))
    c = jnp.cumsum(log_f, axis=1)
    fox_out = sweep_query_blocks(functools.partial(fox_block, k=fk, v=fv, c_k=c), (fq, c))

    dq = dq.reshape(B, L, DIFF_HEADS, 2, DIFF_HEAD_DIM)
    dk = dk.reshape(B, L, DIFF_HEADS, 2, DIFF_HEAD_DIM)
    dv = dv.reshape(B, L, DIFF_HEADS, 2 * DIFF_HEAD_DIM)
    f32 = jnp.float32
    lam = (jnp.exp(jnp.sum(lambda_q1.astype(f32) * lambda_k1.astype(f32)))
           - jnp.exp(jnp.sum(lambda_q2.astype(f32) * lambda_k2.astype(f32))) + lambda_init)
    slopes = jnp.exp2(-ALIBI_MAX_BIAS * jnp.arange(1, DIFF_HEADS + 1, dtype=f32) / DIFF_HEADS)
    diff_out = sweep_query_blocks(functools.partial(diff_block, k=dk, v=dv, lam=lam, slopes=slopes), (dq,))
    diff_out = rms_norm(diff_out, diff_subln_g) * (1.0 - lambda_init)

    y_fox = fox_out.reshape(B, L, FOX_WIDTH) @ w_o_fox
    y_diff = diff_out.reshape(B, L, DIFF_V_WIDTH) @ w_o_diff
    merged = jax.nn.sigmoid(g_fox) * y_fox + jax.nn.sigmoid(g_diff) * y_diff
    return merged @ w_out


def setup_inputs(seed: int = 0) -> dict:
    key = jax.random.key(seed)
    ks = jax.random.split(key, 24)
    f32 = jnp.float32

    def dense(k, shape):
        return jax.random.normal(k, shape, f32) * (shape[-2] ** -0.5)

    def gain(k, n):
        return 1.0 + 0.01 * jax.random.normal(k, (DEPTH, n), f32)

    return {
        "x": jax.random.normal(ks[0], (BATCH, SEQ, D_MODEL), f32),
        "meta_tokens": jax.random.normal(ks[1], (N_META, D_MODEL), f32),
        "ff1_pre_g": gain(ks[2], D_MODEL),
        "ff1_w_gate": dense(ks[3], (DEPTH, D_MODEL, D_FF)),
        "ff1_w_up": dense(ks[4], (DEPTH, D_MODEL, D_FF)),
        "ff1_w_down": dense(ks[5], (DEPTH, D_FF, D_MODEL)),
        "ff1_post_g": gain(ks[6], D_MODEL),
        "mix_pre_g": gain(ks[7], D_MODEL),
        "w_in": dense(ks[8], (DEPTH, D_MODEL, IN_COLS)),
        "b_forget": jax.random.uniform(ks[9], (DEPTH, FOX_HEADS), f32, 1.0, 3.0),
        "lambda_q1": 0.1 * jax.random.normal(ks[10], (DEPTH, DIFF_HEAD_DIM), f32),
        "lambda_k1": 0.1 * jax.random.normal(ks[11], (DEPTH, DIFF_HEAD_DIM), f32),
        "lambda_q2": 0.1 * jax.random.normal(ks[12], (DEPTH, DIFF_HEAD_DIM), f32),
        "lambda_k2": 0.1 * jax.random.normal(ks[13], (DEPTH, DIFF_HEAD_DIM), f32),
        "diff_subln_g": gain(ks[14], 2 * DIFF_HEAD_DIM),
        "w_o_fox": dense(ks[15], (DEPTH, FOX_WIDTH, D_MODEL)),
        "w_o_diff": dense(ks[16], (DEPTH, DIFF_V_WIDTH, D_MODEL)),
        "w_out": dense(ks[17], (DEPTH, D_MODEL, D_MODEL)),
        "mix_post_g": gain(ks[18], D_MODEL),
        "ff2_pre_g": gain(ks[19], D_MODEL),
        "ff2_w_gate": dense(ks[20], (DEPTH, D_MODEL, D_FF)),
        "ff2_w_up": dense(ks[21], (DEPTH, D_MODEL, D_FF)),
        "ff2_w_down": dense(ks[22], (DEPTH, D_FF, D_MODEL)),
        "ff2_post_g": gain(ks[23], D_MODEL),
    }


def reference(x, meta_tokens, ff1_pre_g, ff1_w_gate, ff1_w_up, ff1_w_down, ff1_post_g,
              mix_pre_g, w_in, b_forget, lambda_q1, lambda_k1, lambda_q2, lambda_k2, diff_subln_g,
              w_o_fox, w_o_diff, w_out, mix_post_g,
              ff2_pre_g, ff2_w_gate, ff2_w_up, ff2_w_down, ff2_post_g):
    B = x.shape[0]
    meta = jnp.broadcast_to(meta_tokens.astype(x.dtype)[None], (B, N_META, x.shape[-1]))
    h = jnp.concatenate([meta, x], axis=1)
    for layer in range(DEPTH):
        lambda_init = 0.8 - 0.6 * math.exp(-0.3 * layer)
        h = h + 0.5 * rms_norm(swiglu(rms_norm(h, ff1_pre_g[layer]), ff1_w_gate[layer],
                                      ff1_w_up[layer], ff1_w_down[layer]), ff1_post_g[layer])
        mix = hybrid_mixer(rms_norm(h, mix_pre_g[layer]), w_in[layer], b_forget[layer],
                           lambda_q1[layer], lambda_k1[layer], lambda_q2[layer], lambda_k2[layer],
                           diff_subln_g[layer], w_o_fox[layer], w_o_diff[layer], w_out[layer],
                           lambda_init)
        h = h + rms_norm(mix, mix_post_g[layer])
        h = h + 0.5 * rms_norm(swiglu(rms_norm(h, ff2_pre_g[layer]), ff2_w_gate[layer],
                                      ff2_w_up[layer], ff2_w_down[layer]), ff2_post_g[layer])
    return h[:, N_META:]
```

```python
import functools
import math

import jax
import jax.numpy as jnp
from jax import lax
from jax.experimental import pallas as pl
from jax.experimental.pallas import tpu as pltpu

N_META = 16
FOX_HEADS = 16
DIFF_HEADS = 8
HEAD_DIM = 128
RMS_EPS = 1e-6
ALIBI_MAX_BIAS = 8.0

LANES = 128
META_ROWS = 128
VMEM_LIMIT = 56 * 1024 * 1024
ATT_TILE = 256
CUMSUM_TILE = 256

_F32 = jnp.float32
_BF16 = jnp.bfloat16
_NEG_INF = float("-inf")


def _params(*sem):
    return pltpu.CompilerParams(dimension_semantics=sem, vmem_limit_bytes=VMEM_LIMIT)


def _row_tile(m, want):
    return want if m % want == 0 else m


def _rms(x, g):
    return x * lax.rsqrt(jnp.mean(x * x, axis=-1, keepdims=True) + RMS_EPS) * g


def _sigmoid(x):
    return 1.0 / (1.0 + jnp.exp(-x))


def _dot(a, b):
    return jnp.dot(a, b, preferred_element_type=_F32)


def _dot_nt(a, b):
    return lax.dot_general(a, b, (((1,), (1,)), ((), ())), preferred_element_type=_F32)


def _norm_kernel(h_ref, g_ref, o_ref):
    o_ref[...] = _rms(h_ref[...], g_ref[...]).astype(o_ref.dtype)


def _norm(h, g, tm=256):
    m, d = h.shape
    tm = _row_tile(m, tm)
    return pl.pallas_call(
        _norm_kernel,
        grid=(m // tm,),
        in_specs=[pl.BlockSpec((tm, d), lambda i: (i, 0)),
                  pl.BlockSpec((1, d), lambda i: (0, 0))],
        out_specs=pl.BlockSpec((tm, d), lambda i: (i, 0)),
        out_shape=jax.ShapeDtypeStruct((m, d), _BF16),
        compiler_params=_params("parallel"),
        name="norm",
    )(h, g)


def _resid_norm_kernel(h_ref, y_ref, gpost_ref, gpre_ref, hn_ref, xn_ref, *, alpha):
    hn = h_ref[...] + alpha * _rms(y_ref[...], gpost_ref[...])
    hn_ref[...] = hn
    xn_ref[...] = _rms(hn, gpre_ref[...]).astype(xn_ref.dtype)


def _resid_norm(h, y, gpost, gpre, alpha, tm=256):
    m, d = h.shape
    tm = _row_tile(m, tm)
    row = pl.BlockSpec((tm, d), lambda i: (i, 0))
    vec = pl.BlockSpec((1, d), lambda i: (0, 0))
    return pl.pallas_call(
        functools.partial(_resid_norm_kernel, alpha=alpha),
        grid=(m // tm,),
        in_specs=[row, row, vec, vec],
        out_specs=[row, row],
        out_shape=[jax.ShapeDtypeStruct((m, d), _F32), jax.ShapeDtypeStruct((m, d), _BF16)],
        compiler_params=_params("parallel"),
        name="resid_norm",
    )(h, y, gpost, gpre)


def _resid_kernel(h_ref, y_ref, gpost_ref, o_ref, *, alpha):
    o_ref[...] = h_ref[...] + alpha * _rms(y_ref[...], gpost_ref[...])


def _resid(h, y, gpost, alpha, tm=256):
    m, d = h.shape
    tm = _row_tile(m, tm)
    row = pl.BlockSpec((tm, d), lambda i: (i, 0))
    vec = pl.BlockSpec((1, d), lambda i: (0, 0))
    return pl.pallas_call(
        functools.partial(_resid_kernel, alpha=alpha),
        grid=(m // tm,),
        in_specs=[row, row, vec],
        out_specs=row,
        out_shape=jax.ShapeDtypeStruct((m, d), _F32),
        compiler_params=_params("parallel"),
        name="resid",
    )(h, y, gpost)


def _mm_kernel(a_ref, w_ref, o_ref):
    o_ref[...] = _dot(a_ref[...], w_ref[...]).astype(o_ref.dtype)


def _mm(a, w, out_dtype, tm, tn):
    m, k = a.shape
    n = w.shape[1]
    tm = _row_tile(m, tm)
    return pl.pallas_call(
        _mm_kernel,
        grid=(m // tm, n // tn),
        in_specs=[pl.BlockSpec((tm, k), lambda i, j: (i, 0)),
                  pl.BlockSpec((k, tn), lambda i, j: (0, j))],
        out_specs=pl.BlockSpec((tm, tn), lambda i, j: (i, j)),
        out_shape=jax.ShapeDtypeStruct((m, n), out_dtype),
        compiler_params=_params("parallel", "arbitrary"),
        name="mm",
    )(a, w)


def _gateup_kernel(x_ref, wg_ref, wu_ref, o_ref):
    x = x_ref[...]
    g = _dot(x, wg_ref[...])
    u = _dot(x, wu_ref[...])
    o_ref[...] = (g * _sigmoid(g) * u).astype(o_ref.dtype)


def _gateup(x, wg, wu, tm=1024, tn=256):
    m, k = x.shape
    n = wg.shape[1]
    tm = _row_tile(m, tm)
    wspec = pl.BlockSpec((k, tn), lambda i, j: (0, j))
    return pl.pallas_call(
        _gateup_kernel,
        grid=(m // tm, n // tn),
        in_specs=[pl.BlockSpec((tm, k), lambda i, j: (i, 0)), wspec, wspec],
        out_specs=pl.BlockSpec((tm, tn), lambda i, j: (i, j)),
        out_shape=jax.ShapeDtypeStruct((m, n), _BF16),
        compiler_params=_params("parallel", "arbitrary"),
        name="gateup",
    )(x, wg, wu)


def _proj_kernel(u_ref, w_ref, o_ref, *, tn, q_ranges, gate_start, q_scale):
    j = pl.program_id(1)
    acc = _dot(u_ref[...], w_ref[...])
    col = j * tn
    is_q = functools.reduce(jnp.logical_or, [(col >= lo) & (col < hi) for lo, hi in q_ranges])
    is_gate = col >= gate_start

    @pl.when(is_q)
    def _():
        o_ref[...] = (acc * q_scale).astype(o_ref.dtype)

    @pl.when(is_gate)
    def _():
        o_ref[...] = _sigmoid(acc).astype(o_ref.dtype)

    @pl.when(jnp.logical_not(is_q | is_gate))
    def _():
        o_ref[...] = acc.astype(o_ref.dtype)


def _proj(u, w, q_ranges, gate_start, tm=1024, tn=512):
    m, k = u.shape
    n = w.shape[1]
    tm = _row_tile(m, tm)
    return pl.pallas_call(
        functools.partial(_proj_kernel, tn=tn, q_ranges=q_ranges, gate_start=gate_start,
                          q_scale=HEAD_DIM ** -0.5),
        grid=(m // tm, n // tn),
        in_specs=[pl.BlockSpec((tm, k), lambda i, j: (i, 0)),
                  pl.BlockSpec((k, tn), lambda i, j: (0, j))],
        out_specs=pl.BlockSpec((tm, tn), lambda i, j: (i, j)),
        out_shape=jax.ShapeDtypeStruct((m, n), _BF16),
        compiler_params=_params("parallel", "arbitrary"),
        name="proj",
    )(u, w)


def _merge_kernel(af_ref, ad_ref, wf_ref, wd_ref, gf_ref, gd_ref, o_ref):
    yf = _dot(af_ref[...], wf_ref[...])
    yd = _dot(ad_ref[...], wd_ref[...])
    o_ref[...] = (gf_ref[...].astype(_F32) * yf + gd_ref[...].astype(_F32) * yd).astype(o_ref.dtype)


def _merge(fox, diff, w_of, w_od, proj, gf_col, gd_col, tm=1024, tn=512):
    m, k = fox.shape
    n = w_of.shape[1]
    aspec = pl.BlockSpec((tm, k), lambda i, j: (i, 0))
    wspec = pl.BlockSpec((k, tn), lambda i, j: (0, j))
    return pl.pallas_call(
        _merge_kernel,
        grid=(m // tm, n // tn),
        in_specs=[aspec, aspec, wspec, wspec,
                  pl.BlockSpec((tm, tn), lambda i, j: (i, gf_col // tn + j)),
                  pl.BlockSpec((tm, tn), lambda i, j: (i, gd_col // tn + j))],
        out_specs=pl.BlockSpec((tm, tn), lambda i, j: (i, j)),
        out_shape=jax.ShapeDtypeStruct((m, n), _BF16),
        compiler_params=_params("parallel", "arbitrary"),
        name="merge",
    )(fox, diff, w_of, w_od, proj, proj)


def _cumsum_kernel(u_ref, wf_ref, bf_ref, init_ref, c_ref, ct_ref, carry_ref, *, n_valid):
    tm = u_ref.shape[0]

    @pl.when(pl.program_id(1) == 0)
    def _():
        carry_ref[...] = init_ref[...]

    x = _dot(u_ref[...], wf_ref[...]) + bf_ref[...]
    logf = jnp.minimum(x, 0.0) - jnp.log1p(jnp.exp(-jnp.abs(x)))
    row = lax.broadcasted_iota(jnp.int32, (tm, tm), 0)
    col = lax.broadcasted_iota(jnp.int32, (tm, tm), 1)
    if n_valid < tm:
        logf = jnp.where(lax.broadcasted_iota(jnp.int32, (tm, LANES), 0) < n_valid, logf, 0.0)
    tri = (row >= col).astype(_F32)
    c = jnp.dot(tri, logf, precision=lax.Precision.HIGHEST, preferred_element_type=_F32)
    c = c + carry_ref[...]
    c_ref[...] = c
    ct_ref[...] = c.T[:FOX_HEADS, :]
    carry_ref[...] = c[tm - 1:tm, :]


def _forget_cumsum(u, wf, bf, init, nb, n_valid, tm):
    m, k = u.shape
    nt = m // (nb * tm)
    return pl.pallas_call(
        functools.partial(_cumsum_kernel, n_valid=n_valid),
        grid=(nb, nt),
        in_specs=[pl.BlockSpec((tm, k), lambda b, t: (b * nt + t, 0)),
                  pl.BlockSpec((k, LANES), lambda b, t: (0, 0)),
                  pl.BlockSpec((1, LANES), lambda b, t: (0, 0)),
                  pl.BlockSpec((1, LANES), lambda b, t: (0, 0))],
        out_specs=[pl.BlockSpec((tm, LANES), lambda b, t: (b * nt + t, 0)),
                   pl.BlockSpec((FOX_HEADS, tm), lambda b, t: (0, b * nt + t))],
        out_shape=[jax.ShapeDtypeStruct((m, LANES), _F32),
                   jax.ShapeDtypeStruct((FOX_HEADS, m), _F32)],
        scratch_shapes=[pltpu.VMEM((1, LANES), _F32)],
        compiler_params=_params("arbitrary", "arbitrary"),
        name="forget_cumsum",
    )(u, wf, bf, init)


def _online_step(s, v, m, l, acc):
    m_new = jnp.maximum(m, jnp.max(s, axis=1, keepdims=True))
    alpha = jnp.exp(m - m_new)
    p = jnp.exp(s - m_new)
    l = alpha * l + jnp.sum(p, axis=1, keepdims=True)
    acc = alpha * acc + _dot(p.astype(v.dtype), v)
    return m_new, l, acc


def _first_step(s, v):
    m = jnp.max(s, axis=1, keepdims=True)
    p = jnp.exp(s - m)
    return m, jnp.sum(p, axis=1, keepdims=True), _dot(p.astype(v.dtype), v)


def _fox_kernel(q_ref, k_ref, v_ref, km_ref, vm_ref, c_ref, ct_ref, cmt_ref, o_ref, *, tq):
    head = pl.program_id(1)
    nq = q_ref.shape[0] // tq
    lane = lax.broadcasted_iota(jnp.int32, (tq, LANES), 1)
    causal = (lax.broadcasted_iota(jnp.int32, (tq, tq), 0)
              >= lax.broadcasted_iota(jnp.int32, (tq, tq), 1))
    meta_valid = lane < N_META

    def q_body(qi, carry):
        qs = pl.multiple_of(qi * tq, tq)
        q = q_ref[pl.ds(qs, tq), :]
        cq = jnp.sum(jnp.where(lane == head, c_ref[pl.ds(qs, tq), :], 0.0), axis=1, keepdims=True)

        s = _dot_nt(q, km_ref[...]) + cq - cmt_ref[0]
        m, l, acc = _first_step(jnp.where(meta_valid, s, _NEG_INF), vm_ref[...])

        def kv_body(ki, mla):
            ks = pl.multiple_of(ki * tq, tq)
            s = _dot_nt(q, k_ref[pl.ds(ks, tq), :]) + cq - ct_ref[ki]
            return _online_step(s, v_ref[pl.ds(ks, tq), :], *mla)

        m, l, acc = lax.fori_loop(0, qi, kv_body, (m, l, acc))
        s = _dot_nt(q, k_ref[pl.ds(qs, tq), :]) + cq - ct_ref[qi]
        m, l, acc = _online_step(jnp.where(causal, s, _NEG_INF), v_ref[pl.ds(qs, tq), :], m, l, acc)
        o_ref[pl.ds(qs, tq), :] = (acc / l).astype(o_ref.dtype)
        return carry

    lax.fori_loop(0, nq, q_body, 0)


def _fox_attention(proj, proj_meta, c, ct, cmt, nb, seq, cols, tq=ATT_TILE):
    m = proj.shape[0]
    qb, kb, vb = (cols[n] // HEAD_DIM for n in ("fq", "fk", "fv"))
    nq = seq // tq
    tok = lambda off: pl.BlockSpec((seq, HEAD_DIM), lambda b, h: (b, off + h))
    meta = lambda off: pl.BlockSpec((META_ROWS, HEAD_DIM), lambda b, h: (0, off + h))
    return pl.pallas_call(
        functools.partial(_fox_kernel, tq=tq),
        grid=(nb, FOX_HEADS),
        in_specs=[tok(qb), tok(kb), tok(vb), meta(kb), meta(vb),
                  pl.BlockSpec((seq, LANES), lambda b, h: (b, 0)),
                  pl.BlockSpec((None, nq, 1, tq), lambda b, h: (h, b, 0, 0)),
                  pl.BlockSpec((None, 1, 1, META_ROWS), lambda b, h: (h, 0, 0, 0))],
        out_specs=pl.BlockSpec((seq, HEAD_DIM), lambda b, h: (b, h)),
        out_shape=jax.ShapeDtypeStruct((m, FOX_HEADS * HEAD_DIM), _BF16),
        compiler_params=_params("parallel", "arbitrary"),
        name="fox_attention",
    )(proj, proj, proj, proj_meta, proj_meta, c, ct, cmt)


def _diff_kernel(q_ref, k_ref, v_ref, km_ref, vm_ref, slope_ref, lq1_ref, lk1_ref, lq2_ref,
                 lk2_ref, g_ref, o_ref, *, tq, lambda_init):
    nq = q_ref.shape[0] // tq
    d = HEAD_DIM
    slope = slope_ref[:, 0:1]
    row = lax.broadcasted_iota(jnp.int32, (tq, tq), 0)
    col = lax.broadcasted_iota(jnp.int32, (tq, tq), 1)
    causal = row >= col
    bias = -slope * (row - col).astype(_F32)
    row_m = lax.broadcasted_iota(jnp.int32, (tq, META_ROWS), 0)
    col_m = lax.broadcasted_iota(jnp.int32, (tq, META_ROWS), 1)
    bias_meta = -slope * (row_m - col_m).astype(_F32)
    meta_valid = col_m < N_META
    lam = (jnp.exp(jnp.sum(lq1_ref[...] * lk1_ref[...], axis=1, keepdims=True))
           - jnp.exp(jnp.sum(lq2_ref[...] * lk2_ref[...], axis=1, keepdims=True)) + lambda_init)
    zero_col = jnp.zeros((tq, 1), jnp.int32)

    def tile_offset(delta):
        return -slope * (zero_col + delta).astype(_F32)

    def q_body(qi, carry):
        qs = pl.multiple_of(qi * tq, tq)
        q = q_ref[pl.ds(qs, tq), :]
        qc = (q[:, :d], q[:, d:])
        km = km_ref[...]
        off = tile_offset(qs + N_META)
        state = []
        for c in range(2):
            s = _dot_nt(qc[c], km[:, c * d:(c + 1) * d]) + bias_meta + off
            state.extend(_first_step(jnp.where(meta_valid, s, _NEG_INF), vm_ref[...]))

        def block(ks, st, masked):
            k = k_ref[pl.ds(ks, tq), :]
            v = v_ref[pl.ds(ks, tq), :]
            off = tile_offset(qs - ks)
            out = []
            for c in range(2):
                s = _dot_nt(qc[c], k[:, c * d:(c + 1) * d]) + bias + off
                if masked:
                    s = jnp.where(causal, s, _NEG_INF)
                out.extend(_online_step(s, v, *st[3 * c:3 * c + 3]))
            return tuple(out)

        state = lax.fori_loop(
            0, qi, lambda ki, st: block(pl.multiple_of(ki * tq, tq), st, False), tuple(state))
        m1, l1, a1, m2, l2, a2 = block(qs, state, True)
        o = a1 / l1 - lam * (a2 / l2)
        y = _rms(o, g_ref[...]) * (1.0 - lambda_init)
        o_ref[pl.ds(qs, tq), :] = y.astype(o_ref.dtype)
        return carry

    lax.fori_loop(0, nq, q_body, 0)


def _diff_attention(proj, proj_meta, slopes, lq1, lk1, lq2, lk2, subln_g, lambda_init, nb, seq,
                    cols, tq=ATT_TILE):
    m = proj.shape[0]
    w = 2 * HEAD_DIM
    qb, kb, vb = (cols[n] // w for n in ("dq", "dk", "dv"))
    tok = lambda off: pl.BlockSpec((seq, w), lambda b, h: (b, off + h))
    meta = lambda off: pl.BlockSpec((META_ROWS, w), lambda b, h: (0, off + h))
    vec = lambda n: pl.BlockSpec((1, n), lambda b, h: (0, 0))
    return pl.pallas_call(
        functools.partial(_diff_kernel, tq=tq, lambda_init=lambda_init),
        grid=(nb, DIFF_HEADS),
        in_specs=[tok(qb), tok(kb), tok(vb), meta(kb), meta(vb),
                  pl.BlockSpec((None, 1, LANES), lambda b, h: (h, 0, 0)),
                  vec(HEAD_DIM), vec(HEAD_DIM), vec(HEAD_DIM), vec(HEAD_DIM), vec(w)],
        out_specs=pl.BlockSpec((seq, w), lambda b, h: (b, h)),
        out_shape=jax.ShapeDtypeStruct((m, DIFF_HEADS * w), _BF16),
        compiler_params=_params("parallel", "arbitrary"),
        name="diff_attention",
    )(proj, proj, proj, proj_meta, proj_meta, slopes, lq1, lk1, lq2, lk2, subln_g)


def _ffn(h, xn, w_gate, w_up, w_down, tm_down):
    hid = _gateup(xn, w_gate, w_up)
    return _mm(hid, w_down, _F32, tm=tm_down, tn=256)


def _regroup_w_in(w_in):
    fw = FOX_HEADS * HEAD_DIM
    dw = DIFF_HEADS * 2 * HEAD_DIM
    d_model = w_in.shape[0]
    f0 = 3 * fw
    main = jnp.concatenate([w_in[:, :f0], w_in[:, f0 + FOX_HEADS:]], axis=1).astype(_BF16)
    wf = jnp.pad(w_in[:, f0:f0 + FOX_HEADS], ((0, 0), (0, LANES - FOX_HEADS))).astype(_BF16)
    names = ("fq", "fk", "fv", "dq", "dk", "dv", "gf", "gd")
    widths = (fw, fw, fw, dw, dw, dw, d_model, d_model)
    cols, off = {}, 0
    for n, wd in zip(names, widths):
        cols[n] = off
        off += wd
    return main, wf, cols


def kernel(x, meta_tokens, ff1_pre_g, ff1_w_gate, ff1_w_up, ff1_w_down, ff1_post_g, mix_pre_g, w_in, b_forget, lambda_q1, lambda_k1, lambda_q2, lambda_k2, diff_subln_g, w_o_fox, w_o_diff, w_out, mix_post_g, ff2_pre_g, ff2_w_gate, ff2_w_up, ff2_w_down, ff2_post_g):
    nb, seq, d_model = x.shape
    depth = w_in.shape[0]
    assert depth == 1, "deeper stacks need the meta-token query path"
    h = x.reshape(nb * seq, d_model)
    hm = jnp.pad(meta_tokens.astype(x.dtype), ((0, META_ROWS - N_META), (0, 0)))
    slopes = jnp.exp2(-ALIBI_MAX_BIAS * jnp.arange(1, DIFF_HEADS + 1, dtype=_F32) / DIFF_HEADS)
    slopes = jnp.broadcast_to(slopes[:, None, None], (DIFF_HEADS, 1, LANES))
    nq = seq // ATT_TILE
    vec = lambda a: a.reshape(1, -1)
    bf = lambda a: a.astype(_BF16)

    for layer in range(depth):
        lambda_init = 0.8 - 0.6 * math.exp(-0.3 * layer)

        wg, wu, wd = bf(ff1_w_gate[layer]), bf(ff1_w_up[layer]), bf(ff1_w_down[layer])
        pre, post, mix_pre = vec(ff1_pre_g[layer]), vec(ff1_post_g[layer]), vec(mix_pre_g[layer])
        y = _ffn(h, _norm(h, pre), wg, wu, wd, tm_down=512)
        h, u = _resid_norm(h, y, post, mix_pre, 0.5)
        ym = _ffn(hm, _norm(hm, pre), wg, wu, wd, tm_down=META_ROWS)
        hm, um = _resid_norm(hm, ym, post, mix_pre, 0.5)

        w_main, w_f, cols = _regroup_w_in(w_in[layer])
        q_ranges = ((cols["fq"], cols["fk"]), (cols["dq"], cols["dk"]))
        proj = _proj(u, w_main, q_ranges, cols["gf"])
        proj_m = _proj(um, w_main, q_ranges, cols["gf"])

        b_f = jnp.pad(b_forget[layer].astype(_F32), (0, LANES - FOX_HEADS)).reshape(1, LANES)
        cm, cmt = _forget_cumsum(um, w_f, b_f, jnp.zeros((1, LANES), _F32), 1, N_META, META_ROWS)
        c, ct = _forget_cumsum(u, w_f, b_f, cm[META_ROWS - 1:], nb, CUMSUM_TILE, CUMSUM_TILE)
        ct = ct.reshape(FOX_HEADS, nb * nq, 1, ATT_TILE)
        cmt = cmt.reshape(FOX_HEADS, 1, 1, META_ROWS)

        fox = _fox_attention(proj, proj_m, c, ct, cmt, nb, seq, cols)
        diff = _diff_attention(proj, proj_m, slopes, vec(lambda_q1[layer]).astype(_F32),
                               vec(lambda_k1[layer]).astype(_F32), vec(lambda_q2[layer]).astype(_F32),
                               vec(lambda_k2[layer]).astype(_F32), vec(diff_subln_g[layer]).astype(_F32),
                               lambda_init, nb, seq, cols)
        merged = _merge(fox, diff, bf(w_o_fox[layer]), bf(w_o_diff[layer]), proj, cols["gf"], cols["gd"])
        mix = _mm(merged, bf(w_out[layer]), _F32, tm=1024, tn=512)

        wg, wu, wd = bf(ff2_w_gate[layer]), bf(ff2_w_up[layer]), bf(ff2_w_down[layer])
        pre2, post2 = vec(ff2_pre_g[layer]), vec(ff2_post_g[layer])
        h, xn = _resid_norm(h, mix, vec(mix_post_g[layer]), pre2, 1.0)
        y = _ffn(h, xn, wg, wu, wd, tm_down=512)
        h = _resid(h, y, post2, 0.5)

    return h.reshape(nb, seq, d_model)
```

```python
import functools
import math

import jax
import jax.numpy as jnp
from jax import lax
from jax.experimental import pallas as pl
from jax.experimental.pallas import tpu as pltpu

N_META = 16
FOX_HEADS = 16
DIFF_HEADS = 8
HEAD_DIM = 128
RMS_EPS = 1e-6
ALIBI_MAX_BIAS = 8.0

LANES = 128
META_ROWS = 128
VMEM_LIMIT = 56 * 1024 * 1024
ATT_TILE = 512
FOX_HEADS_PER_STEP = 2
CUMSUM_TILE = 256

_F32 = jnp.float32
_BF16 = jnp.bfloat16
_NEG_INF = float("-inf")


def _params(*sem):
    return pltpu.CompilerParams(dimension_semantics=sem, vmem_limit_bytes=VMEM_LIMIT)


def _row_tile(m, want):
    return want if m % want == 0 else m


def _rms(x, g):
    return x * lax.rsqrt(jnp.mean(x * x, axis=-1, keepdims=True) + RMS_EPS) * g


def _sigmoid(x):
    return 1.0 / (1.0 + jnp.exp(-x))


def _dot(a, b):
    return jnp.dot(a, b, preferred_element_type=_F32)


def _dot_nt(a, b):
    return lax.dot_general(a, b, (((1,), (1,)), ((), ())), preferred_element_type=_F32)


def _norm_kernel(h_ref, g_ref, o_ref):
    o_ref[...] = _rms(h_ref[...], g_ref[...]).astype(o_ref.dtype)


def _norm(h, g, tm=256):
    m, d = h.shape
    tm = _row_tile(m, tm)
    return pl.pallas_call(
        _norm_kernel,
        grid=(m // tm,),
        in_specs=[pl.BlockSpec((tm, d), lambda i: (i, 0)),
                  pl.BlockSpec((1, d), lambda i: (0, 0))],
        out_specs=pl.BlockSpec((tm, d), lambda i: (i, 0)),
        out_shape=jax.ShapeDtypeStruct((m, d), _BF16),
        compiler_params=_params("parallel"),
        name="norm",
    )(h, g)


def _resid_norm_kernel(h_ref, y_ref, gpost_ref, gpre_ref, hn_ref, xn_ref, *, alpha):
    hn = h_ref[...] + alpha * _rms(y_ref[...], gpost_ref[...])
    hn_ref[...] = hn
    xn_ref[...] = _rms(hn, gpre_ref[...]).astype(xn_ref.dtype)


def _resid_norm(h, y, gpost, gpre, alpha, tm=256):
    m, d = h.shape
    tm = _row_tile(m, tm)
    row = pl.BlockSpec((tm, d), lambda i: (i, 0))
    vec = pl.BlockSpec((1, d), lambda i: (0, 0))
    return pl.pallas_call(
        functools.partial(_resid_norm_kernel, alpha=alpha),
        grid=(m // tm,),
        in_specs=[row, row, vec, vec],
        out_specs=[row, row],
        out_shape=[jax.ShapeDtypeStruct((m, d), _F32), jax.ShapeDtypeStruct((m, d), _BF16)],
        compiler_params=_params("parallel"),
        name="resid_norm",
    )(h, y, gpost, gpre)


def _resid_kernel(h_ref, y_ref, gpost_ref, o_ref, *, alpha):
    o_ref[...] = h_ref[...] + alpha * _rms(y_ref[...], gpost_ref[...])


def _resid(h, y, gpost, alpha, tm=256):
    m, d = h.shape
    tm = _row_tile(m, tm)
    row = pl.BlockSpec((tm, d), lambda i: (i, 0))
    vec = pl.BlockSpec((1, d), lambda i: (0, 0))
    return pl.pallas_call(
        functools.partial(_resid_kernel, alpha=alpha),
        grid=(m // tm,),
        in_specs=[row, row, vec],
        out_specs=row,
        out_shape=jax.ShapeDtypeStruct((m, d), _F32),
        compiler_params=_params("parallel"),
        name="resid",
    )(h, y, gpost)


def _mm_kernel(a_ref, w_ref, o_ref):
    o_ref[...] = _dot(a_ref[...], w_ref[...]).astype(o_ref.dtype)


def _mm(a, w, out_dtype, tm, tn):
    m, k = a.shape
    n = w.shape[1]
    tm = _row_tile(m, tm)
    return pl.pallas_call(
        _mm_kernel,
        grid=(m // tm, n // tn),
        in_specs=[pl.BlockSpec((tm, k), lambda i, j: (i, 0)),
                  pl.BlockSpec((k, tn), lambda i, j: (0, j))],
        out_specs=pl.BlockSpec((tm, tn), lambda i, j: (i, j)),
        out_shape=jax.ShapeDtypeStruct((m, n), out_dtype),
        compiler_params=_params("parallel", "arbitrary"),
        name="mm",
    )(a, w)


def _gateup_kernel(x_ref, wg_ref, wu_ref, o_ref):
    x = x_ref[...]
    g = _dot(x, wg_ref[...])
    u = _dot(x, wu_ref[...])
    o_ref[...] = (g * _sigmoid(g) * u).astype(o_ref.dtype)


def _gateup(x, wg, wu, tm=1024, tn=256):
    m, k = x.shape
    n = wg.shape[1]
    tm = _row_tile(m, tm)
    wspec = pl.BlockSpec((k, tn), lambda i, j: (0, j))
    return pl.pallas_call(
        _gateup_kernel,
        grid=(m // tm, n // tn),
        in_specs=[pl.BlockSpec((tm, k), lambda i, j: (i, 0)), wspec, wspec],
        out_specs=pl.BlockSpec((tm, tn), lambda i, j: (i, j)),
        out_shape=jax.ShapeDtypeStruct((m, n), _BF16),
        compiler_params=_params("parallel", "arbitrary"),
        name="gateup",
    )(x, wg, wu)


def _proj_qkv_kernel(u_ref, w_ref, o_ref, *, tn, q_ranges, q_scale):
    col = pl.program_id(1) * tn
    is_q = functools.reduce(jnp.logical_or, [(col >= lo) & (col < hi) for lo, hi in q_ranges])
    scale = jnp.where(is_q, q_scale, 1.0).astype(_F32)
    o_ref[...] = (_dot(u_ref[...], w_ref[...]) * scale).astype(o_ref.dtype)


def _proj_gate_kernel(u_ref, w_ref, o_ref):
    o_ref[...] = _sigmoid(_dot(u_ref[...], w_ref[...])).astype(o_ref.dtype)


def _proj(u, w, q_ranges, gate_start, want_gates, tm=1024, tn=512):
    m, k = u.shape
    n = w.shape[1]
    tm = _row_tile(m, tm)
    gate_blk = gate_start // tn

    def call(body, n_out, col_blk, name):
        return pl.pallas_call(
            body,
            grid=(m // tm, n_out // tn),
            in_specs=[pl.BlockSpec((tm, k), lambda i, j: (i, 0)),
                      pl.BlockSpec((k, tn), lambda i, j: (0, col_blk + j))],
            out_specs=pl.BlockSpec((tm, tn), lambda i, j: (i, j)),
            out_shape=jax.ShapeDtypeStruct((m, n_out), _BF16),
            compiler_params=_params("parallel", "arbitrary"),
            name=name,
        )(u, w)

    qkv = call(functools.partial(_proj_qkv_kernel, tn=tn, q_ranges=q_ranges,
                                 q_scale=HEAD_DIM ** -0.5), gate_start, 0, "proj_qkv")
    gates = call(_proj_gate_kernel, n - gate_start, gate_blk, "proj_gate") if want_gates else None
    return qkv, gates


def _merge_kernel(af_ref, ad_ref, wf_ref, wd_ref, gf_ref, gd_ref, o_ref):
    yf = _dot(af_ref[...], wf_ref[...])
    yd = _dot(ad_ref[...], wd_ref[...])
    o_ref[...] = (gf_ref[...].astype(_F32) * yf + gd_ref[...].astype(_F32) * yd).astype(o_ref.dtype)


def _merge(fox, diff, w_of, w_od, gates, gf_col, gd_col, tm=1024, tn=512):
    m, k = fox.shape
    n = w_of.shape[1]
    aspec = pl.BlockSpec((tm, k), lambda i, j: (i, 0))
    wspec = pl.BlockSpec((k, tn), lambda i, j: (0, j))
    return pl.pallas_call(
        _merge_kernel,
        grid=(m // tm, n // tn),
        in_specs=[aspec, aspec, wspec, wspec,
                  pl.BlockSpec((tm, tn), lambda i, j: (i, gf_col // tn + j)),
                  pl.BlockSpec((tm, tn), lambda i, j: (i, gd_col // tn + j))],
        out_specs=pl.BlockSpec((tm, tn), lambda i, j: (i, j)),
        out_shape=jax.ShapeDtypeStruct((m, n), _BF16),
        compiler_params=_params("parallel", "arbitrary"),
        name="merge",
    )(fox, diff, w_of, w_od, gates, gates)


def _cumsum_kernel(u_ref, wf_ref, bf_ref, init_ref, c_ref, ct_ref, carry_ref, *, n_valid):
    tm = u_ref.shape[0]

    @pl.when(pl.program_id(1) == 0)
    def _():
        carry_ref[...] = init_ref[...]

    x = _dot(u_ref[...], wf_ref[...]) + bf_ref[...]
    logf = jnp.minimum(x, 0.0) - jnp.log1p(jnp.exp(-jnp.abs(x)))
    row = lax.broadcasted_iota(jnp.int32, (tm, tm), 0)
    col = lax.broadcasted_iota(jnp.int32, (tm, tm), 1)
    if n_valid < tm:
        logf = jnp.where(lax.broadcasted_iota(jnp.int32, (tm, LANES), 0) < n_valid, logf, 0.0)
    tri = (row >= col).astype(_F32)
    c = jnp.dot(tri, logf, precision=lax.Precision.HIGHEST, preferred_element_type=_F32)
    c = c + carry_ref[...]
    c_ref[...] = c
    ct_ref[...] = c.T[:FOX_HEADS, :]
    carry_ref[...] = c[tm - 1:tm, :]


def _forget_cumsum(u, wf, bf, init, nb, n_valid, tm):
    m, k = u.shape
    nt = m // (nb * tm)
    return pl.pallas_call(
        functools.partial(_cumsum_kernel, n_valid=n_valid),
        grid=(nb, nt),
        in_specs=[pl.BlockSpec((tm, k), lambda b, t: (b * nt + t, 0)),
                  pl.BlockSpec((k, LANES), lambda b, t: (0, 0)),
                  pl.BlockSpec((1, LANES), lambda b, t: (0, 0)),
                  pl.BlockSpec((1, LANES), lambda b, t: (0, 0))],
        out_specs=[pl.BlockSpec((tm, LANES), lambda b, t: (b * nt + t, 0)),
                   pl.BlockSpec((FOX_HEADS, tm), lambda b, t: (0, b * nt + t))],
        out_shape=[jax.ShapeDtypeStruct((m, LANES), _F32),
                   jax.ShapeDtypeStruct((FOX_HEADS, m), _F32)],
        scratch_shapes=[pltpu.VMEM((1, LANES), _F32)],
        compiler_params=_params("arbitrary", "arbitrary"),
        name="forget_cumsum",
    )(u, wf, bf, init)


def _dot_tn(a, b):
    return lax.dot_general(a, b, (((0,), (0,)), ((), ())), preferred_element_type=_F32)


def _lane_tile(x, n):
    return jnp.concatenate([x] * (n // LANES), axis=1)


def _flash_step(t, v, rq, m, l, acc, valid=None):
    if valid is not None:
        t = jnp.where(valid, t, _NEG_INF)
    m_new = jnp.maximum(m, jnp.max(t, axis=0, keepdims=True) + rq)
    alpha = jnp.exp(m - m_new)
    p = jnp.exp(t - (m_new - rq))
    l = alpha * l + jnp.sum(p, axis=0, keepdims=True)
    acc = alpha * acc + _dot_tn(v, p.astype(v.dtype))
    return m_new, l, acc


def _flash_init(dv, tq):
    return (jnp.full((1, tq), _NEG_INF, _F32), jnp.zeros((1, tq), _F32), jnp.zeros((dv, tq), _F32))


def _fox_kernel(q_ref, k_ref, v_ref, km_ref, vm_ref, c_ref, cm_ref, ct_ref, o_ref, colb_ref,
                colbm_ref, *, tq, hps):
    nq = q_ref.shape[0] // tq
    d = HEAD_DIM
    heads = [pl.program_id(1) * hps + hh for hh in range(hps)]
    lanes = lambda hh: slice(hh * d, (hh + 1) * d)

    lane = lax.broadcasted_iota(jnp.int32, (tq, LANES), 1)
    lane_m = lax.broadcasted_iota(jnp.int32, (META_ROWS, LANES), 1)
    for hh in range(hps):
        def fill(i, carry, hh=hh):
            rs = pl.multiple_of(i * tq, tq)
            ck = jnp.sum(jnp.where(lane == heads[hh], c_ref[pl.ds(rs, tq), :], 0.0),
                         axis=1, keepdims=True)
            colb_ref[hh, pl.ds(rs, tq), :] = jnp.broadcast_to(-ck, (tq, LANES))
            return carry

        lax.fori_loop(0, nq, fill, 0)
        ckm = jnp.sum(jnp.where(lane_m == heads[hh], cm_ref[...], 0.0), axis=1, keepdims=True)
        colbm_ref[hh] = jnp.broadcast_to(-ckm, (META_ROWS, LANES))

    meta_valid = lax.broadcasted_iota(jnp.int32, (META_ROWS, tq), 0) < N_META
    causal = (lax.broadcasted_iota(jnp.int32, (tq, tq), 1)
              >= lax.broadcasted_iota(jnp.int32, (tq, tq), 0))

    def q_body(qi, carry):
        qs = pl.multiple_of(qi * tq, tq)
        q = [q_ref[pl.ds(qs, tq), lanes(hh)] for hh in range(hps)]
        rq = [ct_ref[hh, qi] for hh in range(hps)]

        state = []
        for hh in range(hps):
            t = _dot_nt(km_ref[:, lanes(hh)], q[hh]) + _lane_tile(colbm_ref[hh], tq)
            state.extend(_flash_step(t, vm_ref[:, lanes(hh)], rq[hh], *_flash_init(d, tq),
                                     valid=meta_valid))

        def block(ks, st, valid):
            out = []
            for hh in range(hps):
                t = (_dot_nt(k_ref[pl.ds(ks, tq), lanes(hh)], q[hh])
                     + _lane_tile(colb_ref[hh, pl.ds(ks, tq), :], tq))
                out.extend(_flash_step(t, v_ref[pl.ds(ks, tq), lanes(hh)], rq[hh],
                                       *st[3 * hh:3 * hh + 3], valid=valid))
            return tuple(out)

        state = lax.fori_loop(
            0, qi, lambda ki, st: block(pl.multiple_of(ki * tq, tq), st, None), tuple(state))
        state = block(qs, state, causal)
        for hh in range(hps):
            _, l, acc = state[3 * hh:3 * hh + 3]
            o_ref[pl.ds(qs, tq), lanes(hh)] = (acc / l).T.astype(o_ref.dtype)
        return carry

    lax.fori_loop(0, nq, q_body, 0)


def _fox_attention(proj, proj_meta, c, cm, ct, nb, seq, cols, tq=ATT_TILE, hps=FOX_HEADS_PER_STEP):
    m = proj.shape[0]
    w = hps * HEAD_DIM
    qb, kb, vb = (cols[n] // w for n in ("fq", "fk", "fv"))
    nq = seq // tq
    tok = lambda off: pl.BlockSpec((seq, w), lambda b, s: (b, off + s))
    meta = lambda off: pl.BlockSpec((META_ROWS, w), lambda b, s: (0, off + s))
    return pl.pallas_call(
        functools.partial(_fox_kernel, tq=tq, hps=hps),
        grid=(nb, FOX_HEADS // hps),
        in_specs=[tok(qb), tok(kb), tok(vb), meta(kb), meta(vb),
                  pl.BlockSpec((seq, LANES), lambda b, s: (b, 0)),
                  pl.BlockSpec((META_ROWS, LANES), lambda b, s: (0, 0)),
                  pl.BlockSpec((hps, nq, 1, tq), lambda b, s: (s, b, 0, 0))],
        out_specs=pl.BlockSpec((seq, w), lambda b, s: (b, s)),
        out_shape=jax.ShapeDtypeStruct((m, FOX_HEADS * HEAD_DIM), _BF16),
        scratch_shapes=[pltpu.VMEM((hps, seq, LANES), _F32),
                        pltpu.VMEM((hps, META_ROWS, LANES), _F32)],
        compiler_params=_params("parallel", "arbitrary"),
        name="fox_attention",
    )(proj, proj, proj, proj_meta, proj_meta, c, cm, ct)


def _diff_kernel(q_ref, k_ref, v_ref, km_ref, vm_ref, slope_ref, lq1_ref, lk1_ref, lq2_ref,
                 lk2_ref, g_ref, o_ref, colb_ref, *, tq, lambda_init):
    nq = q_ref.shape[0] // tq
    d = HEAD_DIM
    comp = lambda c: slice(c * d, (c + 1) * d)
    slope = slope_ref[:, 0:1]

    key_row = lax.broadcasted_iota(jnp.int32, (tq, LANES), 0)

    def fill(i, carry):
        rs = pl.multiple_of(i * tq, tq)
        colb_ref[pl.ds(rs, tq), :] = slope * (key_row + (rs + N_META)).astype(_F32)
        return carry

    lax.fori_loop(0, nq, fill, 0)
    colbm = slope * lax.broadcasted_iota(jnp.int32, (META_ROWS, LANES), 0).astype(_F32)
    query_col = lax.broadcasted_iota(jnp.int32, (1, tq), 1)

    meta_valid = lax.broadcasted_iota(jnp.int32, (META_ROWS, tq), 0) < N_META
    causal = (lax.broadcasted_iota(jnp.int32, (tq, tq), 1)
              >= lax.broadcasted_iota(jnp.int32, (tq, tq), 0))
    lam = (jnp.exp(jnp.sum(lq1_ref[...] * lk1_ref[...], axis=1, keepdims=True))
           - jnp.exp(jnp.sum(lq2_ref[...] * lk2_ref[...], axis=1, keepdims=True)) + lambda_init)

    def q_body(qi, carry):
        qs = pl.multiple_of(qi * tq, tq)
        q = [q_ref[pl.ds(qs, tq), comp(c)] for c in range(2)]
        rq = -slope * (query_col + (qs + N_META)).astype(_F32)

        state = []
        for c in range(2):
            t = _dot_nt(km_ref[:, comp(c)], q[c]) + _lane_tile(colbm, tq)
            state.extend(_flash_step(t, vm_ref[...], rq, *_flash_init(2 * d, tq), valid=meta_valid))

        def block(ks, st, valid):
            v = v_ref[pl.ds(ks, tq), :]
            colb = _lane_tile(colb_ref[pl.ds(ks, tq), :], tq)
            out = []
            for c in range(2):
                t = _dot_nt(k_ref[pl.ds(ks, tq), comp(c)], q[c]) + colb
                out.extend(_flash_step(t, v, rq, *st[3 * c:3 * c + 3], valid=valid))
            return tuple(out)

        state = lax.fori_loop(
            0, qi, lambda ki, st: block(pl.multiple_of(ki * tq, tq), st, None), tuple(state))
        _, l1, a1, _, l2, a2 = block(qs, state, causal)
        o = (a1 / l1 - lam * (a2 / l2)).T
        y = _rms(o, g_ref[...]) * (1.0 - lambda_init)
        o_ref[pl.ds(qs, tq), :] = y.astype(o_ref.dtype)
        return carry

    lax.fori_loop(0, nq, q_body, 0)


def _diff_attention(proj, proj_meta, slopes, lq1, lk1, lq2, lk2, subln_g, lambda_init, nb, seq,
                    cols, tq=ATT_TILE):
    m = proj.shape[0]
    w = 2 * HEAD_DIM
    qb, kb, vb = (cols[n] // w for n in ("dq", "dk", "dv"))
    tok = lambda off: pl.BlockSpec((seq, w), lambda b, h: (b, off + h))
    meta = lambda off: pl.BlockSpec((META_ROWS, w), lambda b, h: (0, off + h))
    vec = lambda n: pl.BlockSpec((1, n), lambda b, h: (0, 0))
    return pl.pallas_call(
        functools.partial(_diff_kernel, tq=tq, lambda_init=lambda_init),
        grid=(nb, DIFF_HEADS),
        in_specs=[tok(qb), tok(kb), tok(vb), meta(kb), meta(vb),
                  pl.BlockSpec((None, 1, LANES), lambda b, h: (h, 0, 0)),
                  vec(HEAD_DIM), vec(HEAD_DIM), vec(HEAD_DIM), vec(HEAD_DIM), vec(w)],
        out_specs=pl.BlockSpec((seq, w), lambda b, h: (b, h)),
        out_shape=jax.ShapeDtypeStruct((m, DIFF_HEADS * w), _BF16),
        scratch_shapes=[pltpu.VMEM((seq, LANES), _F32)],
        compiler_params=_params("parallel", "arbitrary"),
        name="diff_attention",
    )(proj, proj, proj, proj_meta, proj_meta, slopes, lq1, lk1, lq2, lk2, subln_g)


def _ffn(h, xn, w_gate, w_up, w_down, tm_down):
    hid = _gateup(xn, w_gate, w_up)
    return _mm(hid, w_down, _F32, tm=tm_down, tn=256)


def _regroup_w_in(w_in):
    fw = FOX_HEADS * HEAD_DIM
    dw = DIFF_HEADS * 2 * HEAD_DIM
    d_model = w_in.shape[0]
    f0 = 3 * fw
    main = jnp.concatenate([w_in[:, :f0], w_in[:, f0 + FOX_HEADS:]], axis=1).astype(_BF16)
    wf = jnp.pad(w_in[:, f0:f0 + FOX_HEADS], ((0, 0), (0, LANES - FOX_HEADS))).astype(_BF16)
    names = ("fq", "fk", "fv", "dq", "dk", "dv", "gf", "gd")
    widths = (fw, fw, fw, dw, dw, dw, d_model, d_model)
    cols, off = {}, 0
    for n, wd in zip(names, widths):
        cols[n] = off
        off += wd
    return main, wf, cols


def kernel(x, meta_tokens, ff1_pre_g, ff1_w_gate, ff1_w_up, ff1_w_down, ff1_post_g, mix_pre_g, w_in, b_forget, lambda_q1, lambda_k1, lambda_q2, lambda_k2, diff_subln_g, w_o_fox, w_o_diff, w_out, mix_post_g, ff2_pre_g, ff2_w_gate, ff2_w_up, ff2_w_down, ff2_post_g):
    nb, seq, d_model = x.shape
    depth = w_in.shape[0]
    assert depth == 1, "deeper stacks need the meta-token query path"
    h = x.reshape(nb * seq, d_model)
    hm = jnp.pad(meta_tokens.astype(x.dtype), ((0, META_ROWS - N_META), (0, 0)))
    slopes = jnp.exp2(-ALIBI_MAX_BIAS * jnp.arange(1, DIFF_HEADS + 1, dtype=_F32) / DIFF_HEADS)
    slopes = jnp.broadcast_to(slopes[:, None, None], (DIFF_HEADS, 1, LANES))
    nq = seq // ATT_TILE
    vec = lambda a: a.reshape(1, -1)
    bf = lambda a: a.astype(_BF16)

    for layer in range(depth):
        lambda_init = 0.8 - 0.6 * math.exp(-0.3 * layer)

        wg, wu, wd = bf(ff1_w_gate[layer]), bf(ff1_w_up[layer]), bf(ff1_w_down[layer])
        pre, post, mix_pre = vec(ff1_pre_g[layer]), vec(ff1_post_g[layer]), vec(mix_pre_g[layer])
        y = _ffn(h, _norm(h, pre), wg, wu, wd, tm_down=512)
        h, u = _resid_norm(h, y, post, mix_pre, 0.5)
        ym = _ffn(hm, _norm(hm, pre), wg, wu, wd, tm_down=META_ROWS)
        hm, um = _resid_norm(hm, ym, post, mix_pre, 0.5)

        w_main, w_f, cols = _regroup_w_in(w_in[layer])
        q_ranges = ((cols["fq"], cols["fk"]), (cols["dq"], cols["dk"]))
        proj, gates = _proj(u, w_main, q_ranges, cols["gf"], True)
        proj_m, _ = _proj(um, w_main, q_ranges, cols["gf"], False)

        b_f = jnp.pad(b_forget[layer].astype(_F32), (0, LANES - FOX_HEADS)).reshape(1, LANES)
        cm, _ = _forget_cumsum(um, w_f, b_f, jnp.zeros((1, LANES), _F32), 1, N_META, META_ROWS)
        c, ct = _forget_cumsum(u, w_f, b_f, cm[META_ROWS - 1:], nb, CUMSUM_TILE, CUMSUM_TILE)
        ct = ct.reshape(FOX_HEADS, nb * nq, 1, ATT_TILE)

        fox = _fox_attention(proj, proj_m, c, cm, ct, nb, seq, cols)
        diff = _diff_attention(proj, proj_m, slopes, vec(lambda_q1[layer]).astype(_F32),
                               vec(lambda_k1[layer]).astype(_F32), vec(lambda_q2[layer]).astype(_F32),
                               vec(lambda_k2[layer]).astype(_F32), vec(diff_subln_g[layer]).astype(_F32),
                               lambda_init, nb, seq, cols)
        merged = _merge(fox, diff, bf(w_o_fox[layer]), bf(w_o_diff[layer]), gates,
                        0, cols["gd"] - cols["gf"])
        mix = _mm(merged, bf(w_out[layer]), _F32, tm=1024, tn=512)

        wg, wu, wd = bf(ff2_w_gate[layer]), bf(ff2_w_up[layer]), bf(ff2_w_down[layer])
        pre2, post2 = vec(ff2_pre_g[layer]), vec(ff2_post_g[layer])
        h, xn = _resid_norm(h, mix, vec(mix_post_g[layer]), pre2, 1.0)
        y = _ffn(h, xn, wg, wu, wd, tm_down=512)
        h = _resid(h, y, post2, 0.5)

    return h.reshape(nb, seq, d_model)
```

```python
import functools
import math

import jax
import jax.numpy as jnp
from jax import lax
from jax.experimental import pallas as pl
from jax.experimental.pallas import tpu as pltpu

N_META = 16
FOX_HEADS = 16
DIFF_HEADS = 8
HEAD_DIM = 128
RMS_EPS = 1e-6
ALIBI_MAX_BIAS = 8.0

LANES = 128
META_ROWS = 128
VMEM_LIMIT = 56 * 1024 * 1024
ATT_TILE = 512
FOX_HEADS_PER_STEP = 2
LOG2E = math.log2(math.e)
CUMSUM_TILE = 256

_F32 = jnp.float32
_BF16 = jnp.bfloat16
_NEG_INF = float("-inf")


def _params(*sem):
    return pltpu.CompilerParams(dimension_semantics=sem, vmem_limit_bytes=VMEM_LIMIT)


def _row_tile(m, want):
    return want if m % want == 0 else m


def _rms(x, g):
    return x * lax.rsqrt(jnp.mean(x * x, axis=-1, keepdims=True) + RMS_EPS) * g


def _sigmoid(x):
    return 1.0 / (1.0 + jnp.exp(-x))


def _dot(a, b):
    return jnp.dot(a, b, preferred_element_type=_F32)


def _dot_nt(a, b):
    return lax.dot_general(a, b, (((1,), (1,)), ((), ())), preferred_element_type=_F32)


def _norm_kernel(h_ref, g_ref, o_ref):
    o_ref[...] = _rms(h_ref[...], g_ref[...]).astype(o_ref.dtype)


def _norm(h, g, tm=256):
    m, d = h.shape
    tm = _row_tile(m, tm)
    return pl.pallas_call(
        _norm_kernel,
        grid=(m // tm,),
        in_specs=[pl.BlockSpec((tm, d), lambda i: (i, 0)),
                  pl.BlockSpec((1, d), lambda i: (0, 0))],
        out_specs=pl.BlockSpec((tm, d), lambda i: (i, 0)),
        out_shape=jax.ShapeDtypeStruct((m, d), _BF16),
        compiler_params=_params("parallel"),
        name="norm",
    )(h, g)


def _resid_norm_kernel(h_ref, y_ref, gpost_ref, gpre_ref, hn_ref, xn_ref, *, alpha):
    hn = h_ref[...] + alpha * _rms(y_ref[...], gpost_ref[...])
    hn_ref[...] = hn
    xn_ref[...] = _rms(hn, gpre_ref[...]).astype(xn_ref.dtype)


def _resid_norm(h, y, gpost, gpre, alpha, tm=256):
    m, d = h.shape
    tm = _row_tile(m, tm)
    row = pl.BlockSpec((tm, d), lambda i: (i, 0))
    vec = pl.BlockSpec((1, d), lambda i: (0, 0))
    return pl.pallas_call(
        functools.partial(_resid_norm_kernel, alpha=alpha),
        grid=(m // tm,),
        in_specs=[row, row, vec, vec],
        out_specs=[row, row],
        out_shape=[jax.ShapeDtypeStruct((m, d), _F32), jax.ShapeDtypeStruct((m, d), _BF16)],
        compiler_params=_params("parallel"),
        name="resid_norm",
    )(h, y, gpost, gpre)


def _resid_kernel(h_ref, y_ref, gpost_ref, o_ref, *, alpha):
    o_ref[...] = h_ref[...] + alpha * _rms(y_ref[...], gpost_ref[...])


def _resid(h, y, gpost, alpha, tm=256):
    m, d = h.shape
    tm = _row_tile(m, tm)
    row = pl.BlockSpec((tm, d), lambda i: (i, 0))
    vec = pl.BlockSpec((1, d), lambda i: (0, 0))
    return pl.pallas_call(
        functools.partial(_resid_kernel, alpha=alpha),
        grid=(m // tm,),
        in_specs=[row, row, vec],
        out_specs=row,
        out_shape=jax.ShapeDtypeStruct((m, d), _F32),
        compiler_params=_params("parallel"),
        name="resid",
    )(h, y, gpost)


def _mm_kernel(a_ref, w_ref, o_ref):
    o_ref[...] = _dot(a_ref[...], w_ref[...]).astype(o_ref.dtype)


def _mm(a, w, out_dtype, tm, tn):
    m, k = a.shape
    n = w.shape[1]
    tm = _row_tile(m, tm)
    return pl.pallas_call(
        _mm_kernel,
        grid=(m // tm, n // tn),
        in_specs=[pl.BlockSpec((tm, k), lambda i, j: (i, 0)),
                  pl.BlockSpec((k, tn), lambda i, j: (0, j))],
        out_specs=pl.BlockSpec((tm, tn), lambda i, j: (i, j)),
        out_shape=jax.ShapeDtypeStruct((m, n), out_dtype),
        compiler_params=_params("parallel", "arbitrary"),
        name="mm",
    )(a, w)


def _gateup_kernel(x_ref, wg_ref, wu_ref, o_ref):
    x = x_ref[...]
    g = _dot(x, wg_ref[...])
    u = _dot(x, wu_ref[...])
    o_ref[...] = (g * _sigmoid(g) * u).astype(o_ref.dtype)


def _gateup(x, wg, wu, tm=1024, tn=256):
    m, k = x.shape
    n = wg.shape[1]
    tm = _row_tile(m, tm)
    wspec = pl.BlockSpec((k, tn), lambda i, j: (0, j))
    return pl.pallas_call(
        _gateup_kernel,
        grid=(m // tm, n // tn),
        in_specs=[pl.BlockSpec((tm, k), lambda i, j: (i, 0)), wspec, wspec],
        out_specs=pl.BlockSpec((tm, tn), lambda i, j: (i, j)),
        out_shape=jax.ShapeDtypeStruct((m, n), _BF16),
        compiler_params=_params("parallel", "arbitrary"),
        name="gateup",
    )(x, wg, wu)


def _proj_qkv_kernel(u_ref, w_ref, o_ref, *, tn, q_cols, q_scale):
    scale = jnp.where(pl.program_id(1) * tn < q_cols, q_scale, 1.0).astype(_F32)
    o_ref[...] = (_dot(u_ref[...], w_ref[...]) * scale).astype(o_ref.dtype)


def _proj_gate_kernel(u_ref, w_ref, o_ref):
    o_ref[...] = _sigmoid(_dot(u_ref[...], w_ref[...])).astype(o_ref.dtype)


def _proj(u, w, col0, n_out, q_cols, tm=1024, tn=512):
    m, k = u.shape
    tm = _row_tile(m, tm)
    col_blk = col0 // tn
    assert col_blk * tn == col0
    if q_cols is None:
        body, name = _proj_gate_kernel, "proj_gate"
    else:
        body = functools.partial(_proj_qkv_kernel, tn=tn, q_cols=q_cols,
                                 q_scale=HEAD_DIM ** -0.5 * LOG2E)
        name = "proj_qkv"
    return pl.pallas_call(
        body,
        grid=(m // tm, n_out // tn),
        in_specs=[pl.BlockSpec((tm, k), lambda i, j: (i, 0)),
                  pl.BlockSpec((k, tn), lambda i, j: (0, col_blk + j))],
        out_specs=pl.BlockSpec((tm, tn), lambda i, j: (i, j)),
        out_shape=jax.ShapeDtypeStruct((m, n_out), _BF16),
        compiler_params=_params("parallel", "arbitrary"),
        name=name,
    )(u, w)


def _merge_kernel(af_ref, ad_ref, wf_ref, wd_ref, gf_ref, gd_ref, o_ref):
    yf = _dot(af_ref[...], wf_ref[...])
    yd = _dot(ad_ref[...], wd_ref[...])
    o_ref[...] = (gf_ref[...].astype(_F32) * yf + gd_ref[...].astype(_F32) * yd).astype(o_ref.dtype)


def _merge(fox, diff, w_of, w_od, gates, gf_col, gd_col, tm=1024, tn=512):
    m, k = fox.shape
    n = w_of.shape[1]
    aspec = pl.BlockSpec((tm, k), lambda i, j: (i, 0))
    wspec = pl.BlockSpec((k, tn), lambda i, j: (0, j))
    return pl.pallas_call(
        _merge_kernel,
        grid=(m // tm, n // tn),
        in_specs=[aspec, aspec, wspec, wspec,
                  pl.BlockSpec((tm, tn), lambda i, j: (i, gf_col // tn + j)),
                  pl.BlockSpec((tm, tn), lambda i, j: (i, gd_col // tn + j))],
        out_specs=pl.BlockSpec((tm, tn), lambda i, j: (i, j)),
        out_shape=jax.ShapeDtypeStruct((m, n), _BF16),
        compiler_params=_params("parallel", "arbitrary"),
        name="merge",
    )(fox, diff, w_of, w_od, gates, gates)


def _cumsum_kernel(u_ref, wf_ref, bf_ref, init_ref, c_ref, ct_ref, carry_ref, *, n_valid):
    tm = u_ref.shape[0]

    @pl.when(pl.program_id(1) == 0)
    def _():
        carry_ref[...] = init_ref[...]

    is_head = lax.broadcasted_iota(jnp.int32, wf_ref.shape, 1) < FOX_HEADS
    wf = jnp.where(is_head, wf_ref[...], 0.0).astype(u_ref.dtype)
    x = _dot(u_ref[...], wf) + bf_ref[...]
    logf = jnp.minimum(x, 0.0) - jnp.log1p(jnp.exp(-jnp.abs(x)))
    row = lax.broadcasted_iota(jnp.int32, (tm, tm), 0)
    col = lax.broadcasted_iota(jnp.int32, (tm, tm), 1)
    if n_valid < tm:
        logf = jnp.where(lax.broadcasted_iota(jnp.int32, (tm, LANES), 0) < n_valid, logf, 0.0)
    tri = (row >= col).astype(_F32)
    c = jnp.dot(tri, logf, precision=lax.Precision.HIGHEST, preferred_element_type=_F32)
    c = c + carry_ref[...]
    c_ref[...] = c
    ct_ref[...] = c.T[:FOX_HEADS, :]
    carry_ref[...] = c[tm - 1:tm, :]


def _forget_cumsum(u, w_in, f_col, bf, init, nb, n_valid, tm):
    m, k = u.shape
    nt = m // (nb * tm)
    f_blk = f_col // LANES
    assert f_blk * LANES == f_col
    return pl.pallas_call(
        functools.partial(_cumsum_kernel, n_valid=n_valid),
        grid=(nb, nt),
        in_specs=[pl.BlockSpec((tm, k), lambda b, t: (b * nt + t, 0)),
                  pl.BlockSpec((k, LANES), lambda b, t: (0, f_blk)),
                  pl.BlockSpec((1, LANES), lambda b, t: (0, 0)),
                  pl.BlockSpec((1, LANES), lambda b, t: (0, 0))],
        out_specs=[pl.BlockSpec((tm, LANES), lambda b, t: (b * nt + t, 0)),
                   pl.BlockSpec((FOX_HEADS, tm), lambda b, t: (0, b * nt + t))],
        out_shape=[jax.ShapeDtypeStruct((m, LANES), _F32),
                   jax.ShapeDtypeStruct((FOX_HEADS, m), _F32)],
        scratch_shapes=[pltpu.VMEM((1, LANES), _F32)],
        compiler_params=_params("arbitrary", "arbitrary"),
        name="forget_cumsum",
    )(u, w_in, bf, init)


def _dot_tn(a, b):
    return lax.dot_general(a, b, (((0,), (0,)), ((), ())), preferred_element_type=_F32)


def _lane_tile(x, n):
    return jnp.concatenate([x] * (n // LANES), axis=1)


def _flash_step(t, v, rq, m, l, acc):
    m_new = jnp.maximum(m, jnp.max(t, axis=0, keepdims=True) + rq)
    alpha = jnp.exp2(m - m_new)
    p = jnp.exp2(t - (m_new - rq))
    l = alpha * l + jnp.sum(p, axis=0, keepdims=True)
    acc = alpha * acc + _dot_tn(v, p.astype(v.dtype))
    return m_new, l, acc


def _flash_init(dv, tq):
    return (jnp.full((1, tq), _NEG_INF, _F32), jnp.zeros((1, tq), _F32), jnp.zeros((dv, tq), _F32))


class _QueryTile:
    def __init__(self, streams, dv, tq, rq, first, scores, values, t_ref):
        self.streams, self.dv, self.tq, self.rq = streams, dv, tq, rq
        self.first, self.scores, self.values, self.t_ref = first, scores, values, t_ref

    def start(self):
        for s in range(self.streams):
            self.t_ref[0, s] = self.scores(s, 0)
        state = []
        for s in range(self.streams):
            state.extend(_flash_step(*self.first(s), self.rq[s], *_flash_init(self.dv, self.tq)))
        return tuple(state)

    def run_pairs(self, n_pairs, state):
        def half(st, slot, blk, nxt):
            for s in range(self.streams):
                self.t_ref[1 - slot, s] = self.scores(s, nxt)
            out = []
            for s in range(self.streams):
                out.extend(_flash_step(self.t_ref[slot, s], self.values(s, blk), self.rq[s],
                                       *st[3 * s:3 * s + 3]))
            return tuple(out)

        def pair(j, st):
            st = half(st, 0, 2 * j, 2 * j + 1)
            return half(st, 1, 2 * j + 1, jnp.minimum(2 * j + 2, 2 * n_pairs - 1))

        return lax.fori_loop(0, n_pairs, pair, state)


def _sweep_tile_pairs(n_tiles, make_tile, finish):
    def body(a, carry):
        tiles = [make_tile(2 * a, 0, None), make_tile(2 * a + 1, 1, 2 * a)]
        states = [t.start() for t in tiles]
        states = [t.run_pairs(a, st) for t, st in zip(tiles, states)]
        for t, st in zip(tiles, states):
            finish(t, st)
        return carry

    lax.fori_loop(0, n_tiles // 2, body, 0)


def _fox_kernel(q_ref, k_ref, v_ref, km_ref, vm_ref, c_ref, cm_ref, ct_ref, o_ref, colb_ref,
                colbm_ref, t_ref, *, tq, hps):
    nq = q_ref.shape[0] // tq
    d = HEAD_DIM
    heads = [pl.program_id(1) * hps + hh for hh in range(hps)]
    lanes = lambda hh: slice(hh * d, (hh + 1) * d)

    lane = lax.broadcasted_iota(jnp.int32, (tq, LANES), 1)
    lane_m = lax.broadcasted_iota(jnp.int32, (META_ROWS, LANES), 1)
    for hh in range(hps):
        def fill(i, carry, hh=hh):
            rs = pl.multiple_of(i * tq, tq)
            ck = jnp.sum(jnp.where(lane == heads[hh], c_ref[pl.ds(rs, tq), :], 0.0),
                         axis=1, keepdims=True)
            colb_ref[hh, pl.ds(rs, tq), :] = jnp.broadcast_to(-LOG2E * ck, (tq, LANES))
            return carry

        lax.fori_loop(0, nq, fill, 0)
        ckm = jnp.sum(jnp.where(lane_m == heads[hh], cm_ref[...], 0.0), axis=1, keepdims=True)
        colbm_ref[hh] = jnp.broadcast_to(-LOG2E * ckm, (META_ROWS, LANES))

    meta_valid = lax.broadcasted_iota(jnp.int32, (META_ROWS, tq), 0) < N_META
    causal = (lax.broadcasted_iota(jnp.int32, (tq, tq), 1)
              >= lax.broadcasted_iota(jnp.int32, (tq, tq), 0))

    def make_tile(qi, slot, odd_block):
        qs = pl.multiple_of(qi * tq, tq)
        q = [q_ref[pl.ds(qs, tq), lanes(hh)] for hh in range(hps)]
        rq = [LOG2E * ct_ref[hh, qi] for hh in range(hps)]

        def scores(hh, j):
            ks = pl.multiple_of(j * tq, tq)
            return (_dot_nt(k_ref[pl.ds(ks, tq), lanes(hh)], q[hh])
                    + _lane_tile(colb_ref[hh, pl.ds(ks, tq), :], tq))

        def values(hh, j):
            return v_ref[pl.ds(pl.multiple_of(j * tq, tq), tq), lanes(hh)]

        def first(hh):
            tm = _dot_nt(km_ref[:, lanes(hh)], q[hh]) + _lane_tile(colbm_ref[hh], tq)
            ts = [jnp.where(causal, scores(hh, qi), _NEG_INF), jnp.where(meta_valid, tm, _NEG_INF)]
            vs = [values(hh, qi), vm_ref[:, lanes(hh)]]
            if odd_block is not None:
                ts.append(scores(hh, odd_block))
                vs.append(values(hh, odd_block))
            return jnp.concatenate(ts, axis=0), jnp.concatenate(vs, axis=0)

        tile = _QueryTile(hps, d, tq, rq, first, scores, values, t_ref.at[slot])
        tile.qs = qs
        return tile

    def finish(tile, state):
        for hh in range(hps):
            _, l, acc = state[3 * hh:3 * hh + 3]
            o_ref[pl.ds(tile.qs, tq), lanes(hh)] = (acc / l).T.astype(o_ref.dtype)

    _sweep_tile_pairs(nq, make_tile, finish)


def _fox_attention(proj, proj_meta, c, cm, ct, nb, seq, tq=ATT_TILE, hps=FOX_HEADS_PER_STEP):
    m = proj.shape[0]
    w = hps * HEAD_DIM
    qb, kb, vb = (i * FOX_HEADS // hps for i in range(3))
    nq = seq // tq
    tok = lambda off: pl.BlockSpec((seq, w), lambda b, s: (b, off + s))
    meta = lambda off: pl.BlockSpec((META_ROWS, w), lambda b, s: (0, off + s))
    return pl.pallas_call(
        functools.partial(_fox_kernel, tq=tq, hps=hps),
        grid=(nb, FOX_HEADS // hps),
        in_specs=[tok(qb), tok(kb), tok(vb), meta(kb), meta(vb),
                  pl.BlockSpec((seq, LANES), lambda b, s: (b, 0)),
                  pl.BlockSpec((META_ROWS, LANES), lambda b, s: (0, 0)),
                  pl.BlockSpec((hps, nq, 1, tq), lambda b, s: (s, b, 0, 0))],
        out_specs=pl.BlockSpec((seq, w), lambda b, s: (b, s)),
        out_shape=jax.ShapeDtypeStruct((m, FOX_HEADS * HEAD_DIM), _BF16),
        scratch_shapes=[pltpu.VMEM((hps, seq, LANES), _F32),
                        pltpu.VMEM((hps, META_ROWS, LANES), _F32),
                        pltpu.VMEM((2, 2, hps, tq, tq), _F32)],
        compiler_params=_params("parallel", "arbitrary"),
        name="fox_attention",
    )(proj, proj, proj, proj_meta, proj_meta, c, cm, ct)


def _diff_kernel(q_ref, k_ref, v_ref, km_ref, vm_ref, slope_ref, lq1_ref, lk1_ref, lq2_ref,
                 lk2_ref, g_ref, o_ref, colb_ref, t_ref, *, tq, lambda_init):
    nq = q_ref.shape[0] // tq
    d = HEAD_DIM
    comp = lambda c: slice(c * d, (c + 1) * d)
    slope2 = LOG2E * slope_ref[:, 0:1]

    key_row = lax.broadcasted_iota(jnp.int32, (tq, LANES), 0)

    def fill(i, carry):
        rs = pl.multiple_of(i * tq, tq)
        colb_ref[pl.ds(rs, tq), :] = slope2 * (key_row + (rs + N_META)).astype(_F32)
        return carry

    lax.fori_loop(0, nq, fill, 0)
    colbm = slope2 * lax.broadcasted_iota(jnp.int32, (META_ROWS, LANES), 0).astype(_F32)
    query_col = lax.broadcasted_iota(jnp.int32, (1, tq), 1)

    meta_valid = lax.broadcasted_iota(jnp.int32, (META_ROWS, tq), 0) < N_META
    causal = (lax.broadcasted_iota(jnp.int32, (tq, tq), 1)
              >= lax.broadcasted_iota(jnp.int32, (tq, tq), 0))
    lam = (jnp.exp(jnp.sum(lq1_ref[...] * lk1_ref[...], axis=1, keepdims=True))
           - jnp.exp(jnp.sum(lq2_ref[...] * lk2_ref[...], axis=1, keepdims=True)) + lambda_init)

    def make_tile(qi, slot, odd_block):
        qs = pl.multiple_of(qi * tq, tq)
        q = [q_ref[pl.ds(qs, tq), comp(c)] for c in range(2)]
        rq = -slope2 * (query_col + (qs + N_META)).astype(_F32)

        def scores(c, j):
            ks = pl.multiple_of(j * tq, tq)
            return (_dot_nt(k_ref[pl.ds(ks, tq), comp(c)], q[c])
                    + _lane_tile(colb_ref[pl.ds(ks, tq), :], tq))

        def values(c, j):
            return v_ref[pl.ds(pl.multiple_of(j * tq, tq), tq), :]

        def first(c):
            tm = _dot_nt(km_ref[:, comp(c)], q[c]) + _lane_tile(colbm, tq)
            ts = [jnp.where(causal, scores(c, qi), _NEG_INF), jnp.where(meta_valid, tm, _NEG_INF)]
            vs = [values(c, qi), vm_ref[...]]
            if odd_block is not None:
                ts.append(scores(c, odd_block))
                vs.append(values(c, odd_block))
            return jnp.concatenate(ts, axis=0), jnp.concatenate(vs, axis=0)

        tile = _QueryTile(2, 2 * d, tq, (rq, rq), first, scores, values, t_ref.at[slot])
        tile.qs = qs
        return tile

    def finish(tile, state):
        _, l1, a1, _, l2, a2 = state
        o = (a1 / l1 - lam * (a2 / l2)).T
        y = _rms(o, g_ref[...]) * (1.0 - lambda_init)
        o_ref[pl.ds(tile.qs, tq), :] = y.astype(o_ref.dtype)

    _sweep_tile_pairs(nq, make_tile, finish)


def _diff_attention(proj, proj_meta, slopes, lq1, lk1, lq2, lk2, subln_g, lambda_init, nb, seq,
                    tq=ATT_TILE):
    m = proj.shape[0]
    w = 2 * HEAD_DIM
    qb, kb, vb = (i * DIFF_HEADS for i in range(3))
    tok = lambda off: pl.BlockSpec((seq, w), lambda b, h: (b, off + h))
    meta = lambda off: pl.BlockSpec((META_ROWS, w), lambda b, h: (0, off + h))
    vec = lambda n: pl.BlockSpec((1, n), lambda b, h: (0, 0))
    return pl.pallas_call(
        functools.partial(_diff_kernel, tq=tq, lambda_init=lambda_init),
        grid=(nb, DIFF_HEADS),
        in_specs=[tok(qb), tok(kb), tok(vb), meta(kb), meta(vb),
                  pl.BlockSpec((None, 1, LANES), lambda b, h: (h, 0, 0)),
                  vec(HEAD_DIM), vec(HEAD_DIM), vec(HEAD_DIM), vec(HEAD_DIM), vec(w)],
        out_specs=pl.BlockSpec((seq, w), lambda b, h: (b, h)),
        out_shape=jax.ShapeDtypeStruct((m, DIFF_HEADS * w), _BF16),
        scratch_shapes=[pltpu.VMEM((seq, LANES), _F32), pltpu.VMEM((2, 2, 2, tq, tq), _F32)],
        compiler_params=_params("parallel", "arbitrary"),
        name="diff_attention",
    )(proj, proj, proj, proj_meta, proj_meta, slopes, lq1, lk1, lq2, lk2, subln_g)


def _ffn(h, xn, w_gate, w_up, w_down, tm_down):
    hid = _gateup(xn, w_gate, w_up)
    return _mm(hid, w_down, _F32, tm=tm_down, tn=256)


def kernel(x, meta_tokens, ff1_pre_g, ff1_w_gate, ff1_w_up, ff1_w_down, ff1_post_g, mix_pre_g, w_in, b_forget, lambda_q1, lambda_k1, lambda_q2, lambda_k2, diff_subln_g, w_o_fox, w_o_diff, w_out, mix_post_g, ff2_pre_g, ff2_w_gate, ff2_w_up, ff2_w_down, ff2_post_g):
    nb, seq, d_model = x.shape
    depth = w_in.shape[0]
    assert depth == 1, "deeper stacks need the meta-token query path"
    h = x.reshape(nb * seq, d_model)
    hm = jnp.pad(meta_tokens.astype(x.dtype), ((0, META_ROWS - N_META), (0, 0)))
    slopes = jnp.exp2(-ALIBI_MAX_BIAS * jnp.arange(1, DIFF_HEADS + 1, dtype=_F32) / DIFF_HEADS)
    slopes = jnp.broadcast_to(slopes[:, None, None], (DIFF_HEADS, 1, LANES))
    nq = seq // ATT_TILE
    fox_w = FOX_HEADS * HEAD_DIM
    diff_w = DIFF_HEADS * 2 * HEAD_DIM
    f_col = 3 * fox_w
    vec = lambda a: a.reshape(1, -1)
    bf = lambda a: a.astype(_BF16)

    for layer in range(depth):
        lambda_init = 0.8 - 0.6 * math.exp(-0.3 * layer)

        wg, wu, wd = bf(ff1_w_gate[layer]), bf(ff1_w_up[layer]), bf(ff1_w_down[layer])
        pre, post, mix_pre = vec(ff1_pre_g[layer]), vec(ff1_post_g[layer]), vec(mix_pre_g[layer])
        y = _ffn(h, _norm(h, pre), wg, wu, wd, tm_down=512)
        h, u = _resid_norm(h, y, post, mix_pre, 0.5)
        ym = _ffn(hm, _norm(hm, pre), wg, wu, wd, tm_down=META_ROWS)
        hm, um = _resid_norm(hm, ym, post, mix_pre, 0.5)

        w = w_in[layer]
        w_fox, w_rest = bf(w[:, :f_col]), bf(w[:, f_col + FOX_HEADS:])
        proj_f = _proj(u, w_fox, 0, 3 * fox_w, fox_w)
        proj_d = _proj(u, w_rest, 0, 3 * diff_w, diff_w)
        gates = _proj(u, w_rest, 3 * diff_w, 2 * d_model, None)
        proj_fm = _proj(um, w_fox, 0, 3 * fox_w, fox_w)
        proj_dm = _proj(um, w_rest, 0, 3 * diff_w, diff_w)

        b_f = jnp.pad(b_forget[layer].astype(_F32), (0, LANES - FOX_HEADS)).reshape(1, LANES)
        cm, _ = _forget_cumsum(um, w, f_col, b_f, jnp.zeros((1, LANES), _F32), 1, N_META, META_ROWS)
        c, ct = _forget_cumsum(u, w, f_col, b_f, cm[META_ROWS - 1:], nb, CUMSUM_TILE, CUMSUM_TILE)
        ct = ct.reshape(FOX_HEADS, nb * nq, 1, ATT_TILE)

        fox = _fox_attention(proj_f, proj_fm, c, cm, ct, nb, seq)
        diff = _diff_attention(proj_d, proj_dm, slopes, vec(lambda_q1[layer]).astype(_F32),
                               vec(lambda_k1[layer]).astype(_F32), vec(lambda_q2[layer]).astype(_F32),
                               vec(lambda_k2[layer]).astype(_F32), vec(diff_subln_g[layer]).astype(_F32),
                               lambda_init, nb, seq)
        merged = _merge(fox, diff, bf(w_o_fox[layer]), bf(w_o_diff[layer]), gates, 0, d_model)
        mix = _mm(merged, bf(w_out[layer]), _F32, tm=1024, tn=512)

        wg, wu, wd = bf(ff2_w_gate[layer]), bf(ff2_w_up[layer]), bf(ff2_w_down[layer])
        pre2, post2 = vec(ff2_pre_g[layer]), vec(ff2_post_g[layer])
        h, xn = _resid_norm(h, mix, vec(mix_post_g[layer]), pre2, 1.0)
        y = _ffn(h, xn, wg, wu, wd, tm_down=512)
        h = _resid(h, y, post2, 0.5)

    return h.reshape(nb, seq, d_model)
```

```python
import functools
import math

import jax
import jax.numpy as jnp
from jax import lax
from jax.experimental import pallas as pl
from jax.experimental.pallas import tpu as pltpu

N_META = 16
FOX_HEADS = 16
DIFF_HEADS = 8
HEAD_DIM = 128
RMS_EPS = 1e-6
ALIBI_MAX_BIAS = 8.0

LANES = 128
META_ROWS = 128
VMEM_LIMIT = 56 * 1024 * 1024
ATT_TILE = 512
FOX_HEADS_PER_STEP = 2
LOG2E = math.log2(math.e)
CUMSUM_TILE = 256

_F32 = jnp.float32
_BF16 = jnp.bfloat16
_NEG_INF = float("-inf")


def _params(*sem):
    return pltpu.CompilerParams(dimension_semantics=sem, vmem_limit_bytes=VMEM_LIMIT)


def _row_tile(m, want):
    return want if m % want == 0 else m


def _rms(x, g):
    return x * lax.rsqrt(jnp.mean(x * x, axis=-1, keepdims=True) + RMS_EPS) * g


def _sigmoid(x):
    return 1.0 / (1.0 + jnp.exp(-x))


def _dot(a, b):
    return jnp.dot(a, b, preferred_element_type=_F32)


def _dot_nt(a, b):
    return lax.dot_general(a, b, (((1,), (1,)), ((), ())), preferred_element_type=_F32)


def _norm_kernel(h_ref, g_ref, o_ref):
    o_ref[...] = _rms(h_ref[...], g_ref[...]).astype(o_ref.dtype)


def _norm(h, g, tm=256):
    m, d = h.shape
    tm = _row_tile(m, tm)
    return pl.pallas_call(
        _norm_kernel,
        grid=(m // tm,),
        in_specs=[pl.BlockSpec((tm, d), lambda i: (i, 0)),
                  pl.BlockSpec((1, d), lambda i: (0, 0))],
        out_specs=pl.BlockSpec((tm, d), lambda i: (i, 0)),
        out_shape=jax.ShapeDtypeStruct((m, d), _BF16),
        compiler_params=_params("parallel"),
        name="norm",
    )(h, g)


def _resid_norm_kernel(h_ref, y_ref, gpost_ref, gpre_ref, hn_ref, xn_ref, *, alpha):
    hn = h_ref[...] + alpha * _rms(y_ref[...], gpost_ref[...])
    hn_ref[...] = hn
    xn_ref[...] = _rms(hn, gpre_ref[...]).astype(xn_ref.dtype)


def _resid_norm(h, y, gpost, gpre, alpha, tm=256):
    m, d = h.shape
    tm = _row_tile(m, tm)
    row = pl.BlockSpec((tm, d), lambda i: (i, 0))
    vec = pl.BlockSpec((1, d), lambda i: (0, 0))
    return pl.pallas_call(
        functools.partial(_resid_norm_kernel, alpha=alpha),
        grid=(m // tm,),
        in_specs=[row, row, vec, vec],
        out_specs=[row, row],
        out_shape=[jax.ShapeDtypeStruct((m, d), _F32), jax.ShapeDtypeStruct((m, d), _BF16)],
        compiler_params=_params("parallel"),
        name="resid_norm",
    )(h, y, gpost, gpre)


def _resid_kernel(h_ref, y_ref, gpost_ref, o_ref, *, alpha):
    o_ref[...] = h_ref[...] + alpha * _rms(y_ref[...], gpost_ref[...])


def _resid(h, y, gpost, alpha, tm=256):
    m, d = h.shape
    tm = _row_tile(m, tm)
    row = pl.BlockSpec((tm, d), lambda i: (i, 0))
    vec = pl.BlockSpec((1, d), lambda i: (0, 0))
    return pl.pallas_call(
        functools.partial(_resid_kernel, alpha=alpha),
        grid=(m // tm,),
        in_specs=[row, row, vec],
        out_specs=row,
        out_shape=jax.ShapeDtypeStruct((m, d), _F32),
        compiler_params=_params("parallel"),
        name="resid",
    )(h, y, gpost)


def _mm_kernel(a_ref, w_ref, o_ref):
    o_ref[...] = _dot(a_ref[...], w_ref[...].astype(a_ref.dtype)).astype(o_ref.dtype)


def _mm(a, w, out_dtype, tm, tn):
    m, k = a.shape
    n = w.shape[1]
    tm = _row_tile(m, tm)
    return pl.pallas_call(
        _mm_kernel,
        grid=(m // tm, n // tn),
        in_specs=[pl.BlockSpec((tm, k), lambda i, j: (i, 0)),
                  pl.BlockSpec((k, tn), lambda i, j: (0, j))],
        out_specs=pl.BlockSpec((tm, tn), lambda i, j: (i, j)),
        out_shape=jax.ShapeDtypeStruct((m, n), out_dtype),
        compiler_params=_params("parallel", "arbitrary"),
        name="mm",
    )(a, w)


def _gateup_kernel(x_ref, wg_ref, wu_ref, o_ref):
    x = x_ref[...]
    g = _dot(x, wg_ref[...].astype(x.dtype))
    u = _dot(x, wu_ref[...].astype(x.dtype))
    o_ref[...] = (g * _sigmoid(g) * u).astype(o_ref.dtype)


def _gateup(x, wg, wu, tm=1024, tn=256):
    m, k = x.shape
    n = wg.shape[1]
    tm = _row_tile(m, tm)
    wspec = pl.BlockSpec((k, tn), lambda i, j: (0, j))
    return pl.pallas_call(
        _gateup_kernel,
        grid=(m // tm, n // tn),
        in_specs=[pl.BlockSpec((tm, k), lambda i, j: (i, 0)), wspec, wspec],
        out_specs=pl.BlockSpec((tm, tn), lambda i, j: (i, j)),
        out_shape=jax.ShapeDtypeStruct((m, n), _BF16),
        compiler_params=_params("parallel", "arbitrary"),
        name="gateup",
    )(x, wg, wu)


def _proj_kernel(u_ref, w_ref, *rest, tn, shift, q_cols, q_scale):
    o_ref = rest[-1]
    w = w_ref[...]
    if shift:
        w = jnp.concatenate([w[:, shift:], rest[0][:, :shift]], axis=1)
    acc = _dot(u_ref[...], w.astype(u_ref.dtype))
    if q_cols is None:
        o_ref[...] = _sigmoid(acc).astype(o_ref.dtype)
    else:
        scale = jnp.where(pl.program_id(1) * tn < q_cols, q_scale, 1.0).astype(_F32)
        o_ref[...] = (acc * scale).astype(o_ref.dtype)


def _proj(u, w_in, col0, n_out, q_cols, tm=1024, tn=512):
    m, k = u.shape
    tm = _row_tile(m, tm)
    shift = col0 % LANES
    base = col0 - shift
    assert base % tn == 0
    in_specs = [pl.BlockSpec((tm, k), lambda i, j: (i, 0)),
                pl.BlockSpec((k, tn), lambda i, j: (0, base // tn + j))]
    if shift:
        in_specs.append(pl.BlockSpec((k, LANES), lambda i, j: (0, (base + tn * (j + 1)) // LANES)))
    return pl.pallas_call(
        functools.partial(_proj_kernel, tn=tn, shift=shift, q_cols=q_cols,
                          q_scale=HEAD_DIM ** -0.5 * LOG2E),
        grid=(m // tm, n_out // tn),
        in_specs=in_specs,
        out_specs=pl.BlockSpec((tm, tn), lambda i, j: (i, j)),
        out_shape=jax.ShapeDtypeStruct((m, n_out), _BF16),
        compiler_params=_params("parallel", "arbitrary"),
        name="proj_gate" if q_cols is None else "proj_qkv",
    )(u, *([w_in] * (len(in_specs) - 1)))


def _merge_kernel(af_ref, ad_ref, wf_ref, wd_ref, gf_ref, gd_ref, o_ref):
    yf = _dot(af_ref[...], wf_ref[...].astype(af_ref.dtype))
    yd = _dot(ad_ref[...], wd_ref[...].astype(ad_ref.dtype))
    o_ref[...] = (gf_ref[...].astype(_F32) * yf + gd_ref[...].astype(_F32) * yd).astype(o_ref.dtype)


def _merge(fox, diff, w_of, w_od, gates, gf_col, gd_col, tm=1024, tn=512):
    m, k = fox.shape
    n = w_of.shape[1]
    aspec = pl.BlockSpec((tm, k), lambda i, j: (i, 0))
    wspec = pl.BlockSpec((k, tn), lambda i, j: (0, j))
    return pl.pallas_call(
        _merge_kernel,
        grid=(m // tm, n // tn),
        in_specs=[aspec, aspec, wspec, wspec,
                  pl.BlockSpec((tm, tn), lambda i, j: (i, gf_col // tn + j)),
                  pl.BlockSpec((tm, tn), lambda i, j: (i, gd_col // tn + j))],
        out_specs=pl.BlockSpec((tm, tn), lambda i, j: (i, j)),
        out_shape=jax.ShapeDtypeStruct((m, n), _BF16),
        compiler_params=_params("parallel", "arbitrary"),
        name="merge",
    )(fox, diff, w_of, w_od, gates, gates)


def _cumsum_kernel(u_ref, wf_ref, bf_ref, init_ref, c_ref, ct_ref, carry_ref, *, n_valid):
    tm = u_ref.shape[0]

    @pl.when(pl.program_id(1) == 0)
    def _():
        carry_ref[...] = init_ref[...]

    is_head = lax.broadcasted_iota(jnp.int32, wf_ref.shape, 1) < FOX_HEADS
    wf = jnp.where(is_head, wf_ref[...], 0.0).astype(u_ref.dtype)
    x = _dot(u_ref[...], wf) + bf_ref[...]
    logf = jnp.minimum(x, 0.0) - jnp.log1p(jnp.exp(-jnp.abs(x)))
    row = lax.broadcasted_iota(jnp.int32, (tm, tm), 0)
    col = lax.broadcasted_iota(jnp.int32, (tm, tm), 1)
    if n_valid < tm:
        logf = jnp.where(lax.broadcasted_iota(jnp.int32, (tm, LANES), 0) < n_valid, logf, 0.0)
    tri = (row >= col).astype(_F32)
    c = jnp.dot(tri, logf, precision=lax.Precision.HIGHEST, preferred_element_type=_F32)
    c = c + carry_ref[...]
    c_ref[...] = c
    ct_ref[...] = c.T[:FOX_HEADS, :]
    carry_ref[...] = c[tm - 1:tm, :]


def _forget_cumsum(u, w_in, f_col, bf, init, nb, n_valid, tm):
    m, k = u.shape
    nt = m // (nb * tm)
    f_blk = f_col // LANES
    assert f_blk * LANES == f_col
    return pl.pallas_call(
        functools.partial(_cumsum_kernel, n_valid=n_valid),
        grid=(nb, nt),
        in_specs=[pl.BlockSpec((tm, k), lambda b, t: (b * nt + t, 0)),
                  pl.BlockSpec((k, LANES), lambda b, t: (0, f_blk)),
                  pl.BlockSpec((1, LANES), lambda b, t: (0, 0)),
                  pl.BlockSpec((1, LANES), lambda b, t: (0, 0))],
        out_specs=[pl.BlockSpec((tm, LANES), lambda b, t: (b * nt + t, 0)),
                   pl.BlockSpec((FOX_HEADS, tm), lambda b, t: (0, b * nt + t))],
        out_shape=[jax.ShapeDtypeStruct((m, LANES), _F32),
                   jax.ShapeDtypeStruct((FOX_HEADS, m), _F32)],
        scratch_shapes=[pltpu.VMEM((1, LANES), _F32)],
        compiler_params=_params("arbitrary", "arbitrary"),
        name="forget_cumsum",
    )(u, w_in, bf, init)


def _dot_tn(a, b):
    return lax.dot_general(a, b, (((0,), (0,)), ((), ())), preferred_element_type=_F32)


def _lane_tile(x, n):
    return jnp.concatenate([x] * (n // LANES), axis=1)


def _flash_step(t, v, rq, m, l, acc):
    m_new = jnp.maximum(m, jnp.max(t, axis=0, keepdims=True) + rq)
    alpha = jnp.exp2(m - m_new)
    p = jnp.exp2(t - (m_new - rq))
    l = alpha * l + jnp.sum(p, axis=0, keepdims=True)
    acc = alpha * acc + _dot_tn(v, p.astype(v.dtype))
    return m_new, l, acc


def _flash_init(dv, tq):
    return (jnp.full((1, tq), _NEG_INF, _F32), jnp.zeros((1, tq), _F32), jnp.zeros((dv, tq), _F32))


class _QueryTile:
    def __init__(self, streams, dv, tq, rq, first, scores, values, t_ref):
        self.streams, self.dv, self.tq, self.rq = streams, dv, tq, rq
        self.first, self.scores, self.values, self.t_ref = first, scores, values, t_ref

    def start(self):
        for s in range(self.streams):
            self.t_ref[0, s] = self.scores(s, 0)
        state = []
        for s in range(self.streams):
            state.extend(_flash_step(*self.first(s), self.rq[s], *_flash_init(self.dv, self.tq)))
        return tuple(state)

    def run_pairs(self, n_pairs, state):
        def half(st, slot, blk, nxt):
            for s in range(self.streams):
                self.t_ref[1 - slot, s] = self.scores(s, nxt)
            out = []
            for s in range(self.streams):
                out.extend(_flash_step(self.t_ref[slot, s], self.values(s, blk), self.rq[s],
                                       *st[3 * s:3 * s + 3]))
            return tuple(out)

        def pair(j, st):
            st = half(st, 0, 2 * j, 2 * j + 1)
            return half(st, 1, 2 * j + 1, jnp.minimum(2 * j + 2, 2 * n_pairs - 1))

        return lax.fori_loop(0, n_pairs, pair, state)


def _sweep_tile_pairs(n_tiles, make_tile, finish):
    def body(a, carry):
        tiles = [make_tile(2 * a, 0, None), make_tile(2 * a + 1, 1, 2 * a)]
        states = [t.start() for t in tiles]
        states = [t.run_pairs(a, st) for t, st in zip(tiles, states)]
        for t, st in zip(tiles, states):
            finish(t, st)
        return carry

    lax.fori_loop(0, n_tiles // 2, body, 0)


def _fox_kernel(q_ref, k_ref, v_ref, km_ref, vm_ref, c_ref, cm_ref, ct_ref, o_ref, colb_ref,
                colbm_ref, t_ref, *, tq, hps):
    nq = q_ref.shape[0] // tq
    d = HEAD_DIM
    heads = [pl.program_id(1) * hps + hh for hh in range(hps)]
    lanes = lambda hh: slice(hh * d, (hh + 1) * d)

    lane = lax.broadcasted_iota(jnp.int32, (tq, LANES), 1)
    lane_m = lax.broadcasted_iota(jnp.int32, (META_ROWS, LANES), 1)
    for hh in range(hps):
        def fill(i, carry, hh=hh):
            rs = pl.multiple_of(i * tq, tq)
            ck = jnp.sum(jnp.where(lane == heads[hh], c_ref[pl.ds(rs, tq), :], 0.0),
                         axis=1, keepdims=True)
            colb_ref[hh, pl.ds(rs, tq), :] = jnp.broadcast_to(-LOG2E * ck, (tq, LANES))
            return carry

        lax.fori_loop(0, nq, fill, 0)
        ckm = jnp.sum(jnp.where(lane_m == heads[hh], cm_ref[...], 0.0), axis=1, keepdims=True)
        colbm_ref[hh] = jnp.broadcast_to(-LOG2E * ckm, (META_ROWS, LANES))

    meta_valid = lax.broadcasted_iota(jnp.int32, (META_ROWS, tq), 0) < N_META
    causal = (lax.broadcasted_iota(jnp.int32, (tq, tq), 1)
              >= lax.broadcasted_iota(jnp.int32, (tq, tq), 0))

    def make_tile(qi, slot, odd_block):
        qs = pl.multiple_of(qi * tq, tq)
        q = [q_ref[pl.ds(qs, tq), lanes(hh)] for hh in range(hps)]
        rq = [LOG2E * ct_ref[hh, qi] for hh in range(hps)]

        def scores(hh, j):
            ks = pl.multiple_of(j * tq, tq)
            return (_dot_nt(k_ref[pl.ds(ks, tq), lanes(hh)], q[hh])
                    + _lane_tile(colb_ref[hh, pl.ds(ks, tq), :], tq))

        def values(hh, j):
            return v_ref[pl.ds(pl.multiple_of(j * tq, tq), tq), lanes(hh)]

        def first(hh):
            tm = _dot_nt(km_ref[:, lanes(hh)], q[hh]) + _lane_tile(colbm_ref[hh], tq)
            ts = [jnp.where(causal, scores(hh, qi), _NEG_INF), jnp.where(meta_valid, tm, _NEG_INF)]
            vs = [values(hh, qi), vm_ref[:, lanes(hh)]]
            if odd_block is not None:
                ts.append(scores(hh, odd_block))
                vs.append(values(hh, odd_block))
            return jnp.concatenate(ts, axis=0), jnp.concatenate(vs, axis=0)

        tile = _QueryTile(hps, d, tq, rq, first, scores, values, t_ref.at[slot])
        tile.qs = qs
        return tile

    def finish(tile, state):
        for hh in range(hps):
            _, l, acc = state[3 * hh:3 * hh + 3]
            o_ref[pl.ds(tile.qs, tq), lanes(hh)] = (acc / l).T.astype(o_ref.dtype)

    _sweep_tile_pairs(nq, make_tile, finish)


def _fox_attention(proj, proj_meta, c, cm, ct, nb, seq, tq=ATT_TILE, hps=FOX_HEADS_PER_STEP):
    m = proj.shape[0]
    w = hps * HEAD_DIM
    qb, kb, vb = (i * FOX_HEADS // hps for i in range(3))
    nq = seq // tq
    tok = lambda off: pl.BlockSpec((seq, w), lambda b, s: (b, off + s))
    meta = lambda off: pl.BlockSpec((META_ROWS, w), lambda b, s: (0, off + s))
    return pl.pallas_call(
        functools.partial(_fox_kernel, tq=tq, hps=hps),
        grid=(nb, FOX_HEADS // hps),
        in_specs=[tok(qb), tok(kb), tok(vb), meta(kb), meta(vb),
                  pl.BlockSpec((seq, LANES), lambda b, s: (b, 0)),
                  pl.BlockSpec((META_ROWS, LANES), lambda b, s: (0, 0)),
                  pl.BlockSpec((hps, nq, 1, tq), lambda b, s: (s, b, 0, 0))],
        out_specs=pl.BlockSpec((seq, w), lambda b, s: (b, s)),
        out_shape=jax.ShapeDtypeStruct((m, FOX_HEADS * HEAD_DIM), _BF16),
        scratch_shapes=[pltpu.VMEM((hps, seq, LANES), _F32),
                        pltpu.VMEM((hps, META_ROWS, LANES), _F32),
                        pltpu.VMEM((2, 2, hps, tq, tq), _F32)],
        compiler_params=_params("parallel", "arbitrary"),
        name="fox_attention",
    )(proj, proj, proj, proj_meta, proj_meta, c, cm, ct)


def _diff_kernel(q_ref, k_ref, v_ref, km_ref, vm_ref, slope_ref, lq1_ref, lk1_ref, lq2_ref,
                 lk2_ref, g_ref, o_ref, colb_ref, t_ref, *, tq, lambda_init):
    nq = q_ref.shape[0] // tq
    d = HEAD_DIM
    comp = lambda c: slice(c * d, (c + 1) * d)
    slope2 = LOG2E * slope_ref[:, 0:1]

    key_row = lax.broadcasted_iota(jnp.int32, (tq, LANES), 0)

    def fill(i, carry):
        rs = pl.multiple_of(i * tq, tq)
        colb_ref[pl.ds(rs, tq), :] = slope2 * (key_row + (rs + N_META)).astype(_F32)
        return carry

    lax.fori_loop(0, nq, fill, 0)
    colbm = slope2 * lax.broadcasted_iota(jnp.int32, (META_ROWS, LANES), 0).astype(_F32)
    query_col = lax.broadcasted_iota(jnp.int32, (1, tq), 1)

    meta_valid = lax.broadcasted_iota(jnp.int32, (META_ROWS, tq), 0) < N_META
    causal = (lax.broadcasted_iota(jnp.int32, (tq, tq), 1)
              >= lax.broadcasted_iota(jnp.int32, (tq, tq), 0))
    lam = (jnp.exp(jnp.sum(lq1_ref[...] * lk1_ref[...], axis=1, keepdims=True))
           - jnp.exp(jnp.sum(lq2_ref[...] * lk2_ref[...], axis=1, keepdims=True)) + lambda_init)

    def make_tile(qi, slot, odd_block):
        qs = pl.multiple_of(qi * tq, tq)
        q = [q_ref[pl.ds(qs, tq), comp(c)] for c in range(2)]
        rq = -slope2 * (query_col + (qs + N_META)).astype(_F32)

        def scores(c, j):
            ks = pl.multiple_of(j * tq, tq)
            return (_dot_nt(k_ref[pl.ds(ks, tq), comp(c)], q[c])
                    + _lane_tile(colb_ref[pl.ds(ks, tq), :], tq))

        def values(c, j):
            return v_ref[pl.ds(pl.multiple_of(j * tq, tq), tq), :]

        def first(c):
            tm = _dot_nt(km_ref[:, comp(c)], q[c]) + _lane_tile(colbm, tq)
            ts = [jnp.where(causal, scores(c, qi), _NEG_INF), jnp.where(meta_valid, tm, _NEG_INF)]
            vs = [values(c, qi), vm_ref[...]]
            if odd_block is not None:
                ts.append(scores(c, odd_block))
                vs.append(values(c, odd_block))
            return jnp.concatenate(ts, axis=0), jnp.concatenate(vs, axis=0)

        tile = _QueryTile(2, 2 * d, tq, (rq, rq), first, scores, values, t_ref.at[slot])
        tile.qs = qs
        return tile

    def finish(tile, state):
        _, l1, a1, _, l2, a2 = state
        o = (a1 / l1 - lam * (a2 / l2)).T
        y = _rms(o, g_ref[...]) * (1.0 - lambda_init)
        o_ref[pl.ds(tile.qs, tq), :] = y.astype(o_ref.dtype)

    _sweep_tile_pairs(nq, make_tile, finish)


def _diff_attention(proj, proj_meta, slopes, lq1, lk1, lq2, lk2, subln_g, lambda_init, nb, seq,
                    tq=ATT_TILE):
    m = proj.shape[0]
    w = 2 * HEAD_DIM
    qb, kb, vb = (i * DIFF_HEADS for i in range(3))
    tok = lambda off: pl.BlockSpec((seq, w), lambda b, h: (b, off + h))
    meta = lambda off: pl.BlockSpec((META_ROWS, w), lambda b, h: (0, off + h))
    vec = lambda n: pl.BlockSpec((1, n), lambda b, h: (0, 0))
    return pl.pallas_call(
        functools.partial(_diff_kernel, tq=tq, lambda_init=lambda_init),
        grid=(nb, DIFF_HEADS),
        in_specs=[tok(qb), tok(kb), tok(vb), meta(kb), meta(vb),
                  pl.BlockSpec((None, 1, LANES), lambda b, h: (h, 0, 0)),
                  vec(HEAD_DIM), vec(HEAD_DIM), vec(HEAD_DIM), vec(HEAD_DIM), vec(w)],
        out_specs=pl.BlockSpec((seq, w), lambda b, h: (b, h)),
        out_shape=jax.ShapeDtypeStruct((m, DIFF_HEADS * w), _BF16),
        scratch_shapes=[pltpu.VMEM((seq, LANES), _F32), pltpu.VMEM((2, 2, 2, tq, tq), _F32)],
        compiler_params=_params("parallel", "arbitrary"),
        name="diff_attention",
    )(proj, proj, proj, proj_meta, proj_meta, slopes, lq1, lk1, lq2, lk2, subln_g)


def _ffn(h, xn, w_gate, w_up, w_down, tm_down):
    hid = _gateup(xn, w_gate, w_up)
    return _mm(hid, w_down, _F32, tm=tm_down, tn=256)


def kernel(x, meta_tokens, ff1_pre_g, ff1_w_gate, ff1_w_up, ff1_w_down, ff1_post_g, mix_pre_g, w_in, b_forget, lambda_q1, lambda_k1, lambda_q2, lambda_k2, diff_subln_g, w_o_fox, w_o_diff, w_out, mix_post_g, ff2_pre_g, ff2_w_gate, ff2_w_up, ff2_w_down, ff2_post_g):
    nb, seq, d_model = x.shape
    depth = w_in.shape[0]
    assert depth == 1, "deeper stacks need the meta-token query path"
    h = x.reshape(nb * seq, d_model)
    hm = jnp.pad(meta_tokens.astype(x.dtype), ((0, META_ROWS - N_META), (0, 0)))
    slopes = jnp.exp2(-ALIBI_MAX_BIAS * jnp.arange(1, DIFF_HEADS + 1, dtype=_F32) / DIFF_HEADS)
    slopes = jnp.broadcast_to(slopes[:, None, None], (DIFF_HEADS, 1, LANES))
    nq = seq // ATT_TILE
    fox_w = FOX_HEADS * HEAD_DIM
    diff_w = DIFF_HEADS * 2 * HEAD_DIM
    f_col = 3 * fox_w
    vec = lambda a: a.reshape(1, -1)
    bf = lambda a: a.astype(_BF16)

    for layer in range(depth):
        lambda_init = 0.8 - 0.6 * math.exp(-0.3 * layer)

        wg, wu, wd = ff1_w_gate[layer], ff1_w_up[layer], bf(ff1_w_down[layer])
        pre, post, mix_pre = vec(ff1_pre_g[layer]), vec(ff1_post_g[layer]), vec(mix_pre_g[layer])
        y = _ffn(h, _norm(h, pre), wg, wu, wd, tm_down=512)
        h, u = _resid_norm(h, y, post, mix_pre, 0.5)
        ym = _ffn(hm, _norm(hm, pre), wg, wu, wd, tm_down=META_ROWS)
        hm, um = _resid_norm(hm, ym, post, mix_pre, 0.5)

        w = w_in[layer]
        d_col = f_col + FOX_HEADS
        proj_f = _proj(u, w, 0, 3 * fox_w, fox_w)
        proj_d = _proj(u, w, d_col, 3 * diff_w, diff_w)
        gates = _proj(u, w, d_col + 3 * diff_w, 2 * d_model, None)
        proj_fm = _proj(um, w, 0, 3 * fox_w, fox_w)
        proj_dm = _proj(um, w, d_col, 3 * diff_w, diff_w)

        b_f = jnp.pad(b_forget[layer].astype(_F32), (0, LANES - FOX_HEADS)).reshape(1, LANES)
        cm, _ = _forget_cumsum(um, w, f_col, b_f, jnp.zeros((1, LANES), _F32), 1, N_META, META_ROWS)
        c, ct = _forget_cumsum(u, w, f_col, b_f, cm[META_ROWS - 1:], nb, CUMSUM_TILE, CUMSUM_TILE)
        ct = ct.reshape(FOX_HEADS, nb * nq, 1, ATT_TILE)

        fox = _fox_attention(proj_f, proj_fm, c, cm, ct, nb, seq)
        diff = _diff_attention(proj_d, proj_dm, slopes, vec(lambda_q1[layer]).astype(_F32),
                               vec(lambda_k1[layer]).astype(_F32), vec(lambda_q2[layer]).astype(_F32),
                               vec(lambda_k2[layer]).astype(_F32), vec(diff_subln_g[layer]).astype(_F32),
                               lambda_init, nb, seq)
        merged = _merge(fox, diff, w_o_fox[layer], w_o_diff[layer], gates, 0, d_model)
        mix = _mm(merged, w_out[layer], _F32, tm=1024, tn=512)

        wg, wu, wd = ff2_w_gate[layer], ff2_w_up[layer], bf(ff2_w_down[layer])
        pre2, post2 = vec(ff2_pre_g[layer]), vec(ff2_post_g[layer])
        h, xn = _resid_norm(h, mix, vec(mix_post_g[layer]), pre2, 1.0)
        y = _ffn(h, xn, wg, wu, wd, tm_down=512)
        h = _resid(h, y, post2, 0.5)

    return h.reshape(nb, seq, d_model)
```

```python
import functools
import math

import jax
import jax.numpy as jnp
from jax import lax
from jax.experimental import pallas as pl
from jax.experimental.pallas import tpu as pltpu

N_META = 16
FOX_HEADS = 16
DIFF_HEADS = 8
HEAD_DIM = 128
RMS_EPS = 1e-6
ALIBI_MAX_BIAS = 8.0

LANES = 128
F32_SUBLANES = 8
META_ROWS = 128
VMEM_LIMIT = 56 * 1024 * 1024
ATT_TILE = 512
FOX_HEADS_PER_STEP = 2
LOG2E = math.log2(math.e)
CUMSUM_TILE = 256

_F32 = jnp.float32
_BF16 = jnp.bfloat16
_NEG_INF = float("-inf")


def _params(*sem):
    return pltpu.CompilerParams(dimension_semantics=sem, vmem_limit_bytes=VMEM_LIMIT)


def _row_tile(m, want):
    return want if m % want == 0 else m


def _rms(x, g):
    return x * lax.rsqrt(jnp.mean(x * x, axis=-1, keepdims=True) + RMS_EPS) * g


def _sigmoid(x):
    return 1.0 / (1.0 + jnp.exp(-x))


def _dot(a, b):
    return jnp.dot(a, b, preferred_element_type=_F32)


def _dot_nt(a, b):
    return lax.dot_general(a, b, (((1,), (1,)), ((), ())), preferred_element_type=_F32)


def _norm_kernel(h_ref, g_ref, o_ref):
    o_ref[...] = _rms(h_ref[...], g_ref[...]).astype(o_ref.dtype)


def _norm(h, g, tm=256):
    m, d = h.shape
    tm = _row_tile(m, tm)
    return pl.pallas_call(
        _norm_kernel,
        grid=(m // tm,),
        in_specs=[pl.BlockSpec((tm, d), lambda i: (i, 0)),
                  pl.BlockSpec((1, d), lambda i: (0, 0))],
        out_specs=pl.BlockSpec((tm, d), lambda i: (i, 0)),
        out_shape=jax.ShapeDtypeStruct((m, d), _BF16),
        compiler_params=_params("parallel"),
        name="norm",
    )(h, g)


def _resid_norm_kernel(h_ref, y_ref, gpost_ref, gpre_ref, hn_ref, xn_ref, *, alpha):
    hn = h_ref[...] + alpha * _rms(y_ref[...], gpost_ref[...])
    hn_ref[...] = hn
    xn_ref[...] = _rms(hn, gpre_ref[...]).astype(xn_ref.dtype)


def _resid_norm(h, y, gpost, gpre, alpha, tm=256):
    m, d = h.shape
    tm = _row_tile(m, tm)
    row = pl.BlockSpec((tm, d), lambda i: (i, 0))
    vec = pl.BlockSpec((1, d), lambda i: (0, 0))
    return pl.pallas_call(
        functools.partial(_resid_norm_kernel, alpha=alpha),
        grid=(m // tm,),
        in_specs=[row, row, vec, vec],
        out_specs=[row, row],
        out_shape=[jax.ShapeDtypeStruct((m, d), _F32), jax.ShapeDtypeStruct((m, d), _BF16)],
        compiler_params=_params("parallel"),
        name="resid_norm",
    )(h, y, gpost, gpre)


def _resid_kernel(h_ref, y_ref, gpost_ref, o_ref, *, alpha):
    o_ref[...] = h_ref[...] + alpha * _rms(y_ref[...], gpost_ref[...])


def _resid(h, y, gpost, alpha, tm=256):
    m, d = h.shape
    tm = _row_tile(m, tm)
    row = pl.BlockSpec((tm, d), lambda i: (i, 0))
    vec = pl.BlockSpec((1, d), lambda i: (0, 0))
    return pl.pallas_call(
        functools.partial(_resid_kernel, alpha=alpha),
        grid=(m // tm,),
        in_specs=[row, row, vec],
        out_specs=row,
        out_shape=jax.ShapeDtypeStruct((m, d), _F32),
        compiler_params=_params("parallel"),
        name="resid",
    )(h, y, gpost)


def _mm_kernel(a_ref, w_ref, o_ref):
    o_ref[...] = _dot(a_ref[...], w_ref[...].astype(a_ref.dtype)).astype(o_ref.dtype)


def _mm(a, w, out_dtype, tm, tn):
    m, k = a.shape
    n = w.shape[1]
    tm = _row_tile(m, tm)
    return pl.pallas_call(
        _mm_kernel,
        grid=(m // tm, n // tn),
        in_specs=[pl.BlockSpec((tm, k), lambda i, j: (i, 0)),
                  pl.BlockSpec((k, tn), lambda i, j: (0, j))],
        out_specs=pl.BlockSpec((tm, tn), lambda i, j: (i, j)),
        out_shape=jax.ShapeDtypeStruct((m, n), out_dtype),
        compiler_params=_params("parallel", "arbitrary"),
        name="mm",
    )(a, w)


def _gateup_kernel(x_ref, wg_ref, wu_ref, o_ref):
    x = x_ref[...]
    g = _dot(x, wg_ref[...].astype(x.dtype))
    u = _dot(x, wu_ref[...].astype(x.dtype))
    o_ref[...] = (g * _sigmoid(g) * u).astype(o_ref.dtype)


def _gateup(x, wg, wu, tm=1024, tn=256):
    m, k = x.shape
    n = wg.shape[1]
    tm = _row_tile(m, tm)
    wspec = pl.BlockSpec((k, tn), lambda i, j: (0, j))
    return pl.pallas_call(
        _gateup_kernel,
        grid=(m // tm, n // tn),
        in_specs=[pl.BlockSpec((tm, k), lambda i, j: (i, 0)), wspec, wspec],
        out_specs=pl.BlockSpec((tm, tn), lambda i, j: (i, j)),
        out_shape=jax.ShapeDtypeStruct((m, n), _BF16),
        compiler_params=_params("parallel", "arbitrary"),
        name="gateup",
    )(x, wg, wu)


def _proj_kernel(u_ref, wt_ref, o_ref, *, tn, q_cols, q_scale):
    acc = _dot_nt(u_ref[...], wt_ref[...].astype(u_ref.dtype))
    if q_cols is None:
        o_ref[...] = _sigmoid(acc).astype(o_ref.dtype)
    else:
        scale = jnp.where(pl.program_id(1) * tn < q_cols, q_scale, 1.0).astype(_F32)
        o_ref[...] = (acc * scale).astype(o_ref.dtype)


def _proj(u, w_in_t, col0, n_out, q_cols, tm=1024, tn=512):
    m, k = u.shape
    tm = _row_tile(m, tm)
    return pl.pallas_call(
        functools.partial(_proj_kernel, tn=tn, q_cols=q_cols, q_scale=HEAD_DIM ** -0.5 * LOG2E),
        grid=(m // tm, n_out // tn),
        in_specs=[pl.BlockSpec((tm, k), lambda i, j: (i, 0)),
                  pl.BlockSpec((pl.Element(tn), pl.Element(k)),
                               lambda i, j: (pl.multiple_of(col0 + j * tn, F32_SUBLANES), 0))],
        out_specs=pl.BlockSpec((tm, tn), lambda i, j: (i, j)),
        out_shape=jax.ShapeDtypeStruct((m, n_out), _BF16),
        compiler_params=_params("parallel", "arbitrary"),
        name="proj_gate" if q_cols is None else "proj_qkv",
    )(u, w_in_t)


def _merge_kernel(af_ref, ad_ref, wf_ref, wd_ref, gf_ref, gd_ref, o_ref):
    yf = _dot(af_ref[...], wf_ref[...].astype(af_ref.dtype))
    yd = _dot(ad_ref[...], wd_ref[...].astype(ad_ref.dtype))
    o_ref[...] = (gf_ref[...].astype(_F32) * yf + gd_ref[...].astype(_F32) * yd).astype(o_ref.dtype)


def _merge(fox, diff, w_of, w_od, gates, gf_col, gd_col, tm=1024, tn=512):
    m, k = fox.shape
    n = w_of.shape[1]
    aspec = pl.BlockSpec((tm, k), lambda i, j: (i, 0))
    wspec = pl.BlockSpec((k, tn), lambda i, j: (0, j))
    return pl.pallas_call(
        _merge_kernel,
        grid=(m // tm, n // tn),
        in_specs=[aspec, aspec, wspec, wspec,
                  pl.BlockSpec((tm, tn), lambda i, j: (i, gf_col // tn + j)),
                  pl.BlockSpec((tm, tn), lambda i, j: (i, gd_col // tn + j))],
        out_specs=pl.BlockSpec((tm, tn), lambda i, j: (i, j)),
        out_shape=jax.ShapeDtypeStruct((m, n), _BF16),
        compiler_params=_params("parallel", "arbitrary"),
        name="merge",
    )(fox, diff, w_of, w_od, gates, gates)


def _cumsum_kernel(u_ref, wf_ref, bf_ref, init_ref, c_ref, ct_ref, carry_ref, *, n_valid):
    tm = u_ref.shape[0]

    @pl.when(pl.program_id(1) == 0)
    def _():
        carry_ref[...] = init_ref[...]

    is_head = lax.broadcasted_iota(jnp.int32, wf_ref.shape, 0) < FOX_HEADS
    wf = jnp.where(is_head, wf_ref[...], 0.0).astype(u_ref.dtype)
    x = _dot_nt(u_ref[...], wf) + bf_ref[...]
    logf = jnp.minimum(x, 0.0) - jnp.log1p(jnp.exp(-jnp.abs(x)))
    row = lax.broadcasted_iota(jnp.int32, (tm, tm), 0)
    col = lax.broadcasted_iota(jnp.int32, (tm, tm), 1)
    if n_valid < tm:
        logf = jnp.where(lax.broadcasted_iota(jnp.int32, (tm, LANES), 0) < n_valid, logf, 0.0)
    tri = (row >= col).astype(_F32)
    c = jnp.dot(tri, logf, precision=lax.Precision.HIGHEST, preferred_element_type=_F32)
    c = c + carry_ref[...]
    c_ref[...] = c
    ct_ref[...] = c.T[:FOX_HEADS, :]
    carry_ref[...] = c[tm - 1:tm, :]


def _forget_cumsum(u, w_in_t, f_col, bf, init, nb, n_valid, tm):
    m, k = u.shape
    nt = m // (nb * tm)
    f_blk = f_col // LANES
    assert f_blk * LANES == f_col
    return pl.pallas_call(
        functools.partial(_cumsum_kernel, n_valid=n_valid),
        grid=(nb, nt),
        in_specs=[pl.BlockSpec((tm, k), lambda b, t: (b * nt + t, 0)),
                  pl.BlockSpec((LANES, k), lambda b, t: (f_blk, 0)),
                  pl.BlockSpec((1, LANES), lambda b, t: (0, 0)),
                  pl.BlockSpec((1, LANES), lambda b, t: (0, 0))],
        out_specs=[pl.BlockSpec((tm, LANES), lambda b, t: (b * nt + t, 0)),
                   pl.BlockSpec((FOX_HEADS, tm), lambda b, t: (0, b * nt + t))],
        out_shape=[jax.ShapeDtypeStruct((m, LANES), _F32),
                   jax.ShapeDtypeStruct((FOX_HEADS, m), _F32)],
        scratch_shapes=[pltpu.VMEM((1, LANES), _F32)],
        compiler_params=_params("arbitrary", "arbitrary"),
        name="forget_cumsum",
    )(u, w_in_t, bf, init)


def _dot_tn(a, b):
    return lax.dot_general(a, b, (((0,), (0,)), ((), ())), preferred_element_type=_F32)


def _lane_tile(x, n):
    return jnp.concatenate([x] * (n // LANES), axis=1)


def _flash_step(t, v, rq, m, l, acc):
    m_new = jnp.maximum(m, jnp.max(t, axis=0, keepdims=True) + rq)
    alpha = jnp.exp2(m - m_new)
    p = jnp.exp2(t - (m_new - rq))
    l = alpha * l + jnp.sum(p, axis=0, keepdims=True)
    acc = alpha * acc + _dot_tn(v, p.astype(v.dtype))
    return m_new, l, acc


def _flash_init(dv, tq):
    return (jnp.full((1, tq), _NEG_INF, _F32), jnp.zeros((1, tq), _F32), jnp.zeros((dv, tq), _F32))


class _QueryTile:
    def __init__(self, streams, dv, tq, rq, first, scores, values, t_ref):
        self.streams, self.dv, self.tq, self.rq = streams, dv, tq, rq
        self.first, self.scores, self.values, self.t_ref = first, scores, values, t_ref

    def start(self):
        for s in range(self.streams):
            self.t_ref[0, s] = self.scores(s, 0)
        state = []
        for s in range(self.streams):
            state.extend(_flash_step(*self.first(s), self.rq[s], *_flash_init(self.dv, self.tq)))
        return tuple(state)

    def run_pairs(self, n_pairs, state):
        def half(st, slot, blk, nxt):
            for s in range(self.streams):
                self.t_ref[1 - slot, s] = self.scores(s, nxt)
            out = []
            for s in range(self.streams):
                out.extend(_flash_step(self.t_ref[slot, s], self.values(s, blk), self.rq[s],
                                       *st[3 * s:3 * s + 3]))
            return tuple(out)

        def pair(j, st):
            st = half(st, 0, 2 * j, 2 * j + 1)
            return half(st, 1, 2 * j + 1, jnp.minimum(2 * j + 2, 2 * n_pairs - 1))

        return lax.fori_loop(0, n_pairs, pair, state)


def _sweep_tile_pairs(n_tiles, make_tile, finish):
    def body(a, carry):
        tiles = [make_tile(2 * a, 0, None), make_tile(2 * a + 1, 1, 2 * a)]
        states = [t.start() for t in tiles]
        states = [t.run_pairs(a, st) for t, st in zip(tiles, states)]
        for t, st in zip(tiles, states):
            finish(t, st)
        return carry

    lax.fori_loop(0, n_tiles // 2, body, 0)


def _fox_kernel(q_ref, k_ref, v_ref, km_ref, vm_ref, c_ref, cm_ref, ct_ref, o_ref, colb_ref,
                colbm_ref, t_ref, *, tq, hps):
    nq = q_ref.shape[0] // tq
    d = HEAD_DIM
    heads = [pl.program_id(1) * hps + hh for hh in range(hps)]
    lanes = lambda hh: slice(hh * d, (hh + 1) * d)

    lane = lax.broadcasted_iota(jnp.int32, (tq, LANES), 1)
    lane_m = lax.broadcasted_iota(jnp.int32, (META_ROWS, LANES), 1)
    for hh in range(hps):
        def fill(i, carry, hh=hh):
            rs = pl.multiple_of(i * tq, tq)
            ck = jnp.sum(jnp.where(lane == heads[hh], c_ref[pl.ds(rs, tq), :], 0.0),
                         axis=1, keepdims=True)
            colb_ref[hh, pl.ds(rs, tq), :] = jnp.broadcast_to(-LOG2E * ck, (tq, LANES))
            return carry

        lax.fori_loop(0, nq, fill, 0)
        ckm = jnp.sum(jnp.where(lane_m == heads[hh], cm_ref[...], 0.0), axis=1, keepdims=True)
        colbm_ref[hh] = jnp.broadcast_to(-LOG2E * ckm, (META_ROWS, LANES))

    meta_valid = lax.broadcasted_iota(jnp.int32, (META_ROWS, tq), 0) < N_META
    causal = (lax.broadcasted_iota(jnp.int32, (tq, tq), 1)
              >= lax.broadcasted_iota(jnp.int32, (tq, tq), 0))

    def make_tile(qi, slot, odd_block):
        qs = pl.multiple_of(qi * tq, tq)
        q = [q_ref[pl.ds(qs, tq), lanes(hh)] for hh in range(hps)]
        rq = [LOG2E * ct_ref[hh, qi] for hh in range(hps)]

        def scores(hh, j):
            ks = pl.multiple_of(j * tq, tq)
            return (_dot_nt(k_ref[pl.ds(ks, tq), lanes(hh)], q[hh])
                    + _lane_tile(colb_ref[hh, pl.ds(ks, tq), :], tq))

        def values(hh, j):
            return v_ref[pl.ds(pl.multiple_of(j * tq, tq), tq), lanes(hh)]

        def first(hh):
            tm = _dot_nt(km_ref[:, lanes(hh)], q[hh]) + _lane_tile(colbm_ref[hh], tq)
            ts = [jnp.where(causal, scores(hh, qi), _NEG_INF), jnp.where(meta_valid, tm, _NEG_INF)]
            vs = [values(hh, qi), vm_ref[:, lanes(hh)]]
            if odd_block is not None:
                ts.append(scores(hh, odd_block))
                vs.append(values(hh, odd_block))
            return jnp.concatenate(ts, axis=0), jnp.concatenate(vs, axis=0)

        tile = _QueryTile(hps, d, tq, rq, first, scores, values, t_ref.at[slot])
        tile.qs = qs
        return tile

    def finish(tile, state):
        for hh in range(hps):
            _, l, acc = state[3 * hh:3 * hh + 3]
            o_ref[pl.ds(tile.qs, tq), lanes(hh)] = (acc / l).T.astype(o_ref.dtype)

    _sweep_tile_pairs(nq, make_tile, finish)


def _fox_attention(proj, proj_meta, c, cm, ct, nb, seq, tq=ATT_TILE, hps=FOX_HEADS_PER_STEP):
    m = proj.shape[0]
    w = hps * HEAD_DIM
    qb, kb, vb = (i * FOX_HEADS // hps for i in range(3))
    nq = seq // tq
    tok = lambda off: pl.BlockSpec((seq, w), lambda b, s: (b, off + s))
    meta = lambda off: pl.BlockSpec((META_ROWS, w), lambda b, s: (0, off + s))
    return pl.pallas_call(
        functools.partial(_fox_kernel, tq=tq, hps=hps),
        grid=(nb, FOX_HEADS // hps),
        in_specs=[tok(qb), tok(kb), tok(vb), meta(kb), meta(vb),
                  pl.BlockSpec((seq, LANES), lambda b, s: (b, 0)),
                  pl.BlockSpec((META_ROWS, LANES), lambda b, s: (0, 0)),
                  pl.BlockSpec((hps, nq, 1, tq), lambda b, s: (s, b, 0, 0))],
        out_specs=pl.BlockSpec((seq, w), lambda b, s: (b, s)),
        out_shape=jax.ShapeDtypeStruct((m, FOX_HEADS * HEAD_DIM), _BF16),
        scratch_shapes=[pltpu.VMEM((hps, seq, LANES), _F32),
                        pltpu.VMEM((hps, META_ROWS, LANES), _F32),
                        pltpu.VMEM((2, 2, hps, tq, tq), _F32)],
        compiler_params=_params("parallel", "arbitrary"),
        name="fox_attention",
    )(proj, proj, proj, proj_meta, proj_meta, c, cm, ct)


def _diff_kernel(q_ref, k_ref, v_ref, km_ref, vm_ref, slope_ref, lq1_ref, lk1_ref, lq2_ref,
                 lk2_ref, g_ref, o_ref, colb_ref, t_ref, *, tq, lambda_init):
    nq = q_ref.shape[0] // tq
    d = HEAD_DIM
    comp = lambda c: slice(c * d, (c + 1) * d)
    slope2 = LOG2E * slope_ref[:, 0:1]

    key_row = lax.broadcasted_iota(jnp.int32, (tq, LANES), 0)

    def fill(i, carry):
        rs = pl.multiple_of(i * tq, tq)
        colb_ref[pl.ds(rs, tq), :] = slope2 * (key_row + (rs + N_META)).astype(_F32)
        return carry

    lax.fori_loop(0, nq, fill, 0)
    colbm = slope2 * lax.broadcasted_iota(jnp.int32, (META_ROWS, LANES), 0).astype(_F32)
    query_col = lax.broadcasted_iota(jnp.int32, (1, tq), 1)

    meta_valid = lax.broadcasted_iota(jnp.int32, (META_ROWS, tq), 0) < N_META
    causal = (lax.broadcasted_iota(jnp.int32, (tq, tq), 1)
              >= lax.broadcasted_iota(jnp.int32, (tq, tq), 0))
    lam = (jnp.exp(jnp.sum(lq1_ref[...] * lk1_ref[...], axis=1, keepdims=True))
           - jnp.exp(jnp.sum(lq2_ref[...] * lk2_ref[...], axis=1, keepdims=True)) + lambda_init)

    def make_tile(qi, slot, odd_block):
        qs = pl.multiple_of(qi * tq, tq)
        q = [q_ref[pl.ds(qs, tq), comp(c)] for c in range(2)]
        rq = -slope2 * (query_col + (qs + N_META)).astype(_F32)

        def scores(c, j):
            ks = pl.multiple_of(j * tq, tq)
            return (_dot_nt(k_ref[pl.ds(ks, tq), comp(c)], q[c])
                    + _lane_tile(colb_ref[pl.ds(ks, tq), :], tq))

        def values(c, j):
            return v_ref[pl.ds(pl.multiple_of(j * tq, tq), tq), :]

        def first(c):
            tm = _dot_nt(km_ref[:, comp(c)], q[c]) + _lane_tile(colbm, tq)
            ts = [jnp.where(causal, scores(c, qi), _NEG_INF), jnp.where(meta_valid, tm, _NEG_INF)]
            vs = [values(c, qi), vm_ref[...]]
            if odd_block is not None:
                ts.append(scores(c, odd_block))
                vs.append(values(c, odd_block))
            return jnp.concatenate(ts, axis=0), jnp.concatenate(vs, axis=0)

        tile = _QueryTile(2, 2 * d, tq, (rq, rq), first, scores, values, t_ref.at[slot])
        tile.qs = qs
        return tile

    def finish(tile, state):
        _, l1, a1, _, l2, a2 = state
        o = (a1 / l1 - lam * (a2 / l2)).T
        y = _rms(o, g_ref[...]) * (1.0 - lambda_init)
        o_ref[pl.ds(tile.qs, tq), :] = y.astype(o_ref.dtype)

    _sweep_tile_pairs(nq, make_tile, finish)


def _diff_attention(proj, proj_meta, slopes, lq1, lk1, lq2, lk2, subln_g, lambda_init, nb, seq,
                    tq=ATT_TILE):
    m = proj.shape[0]
    w = 2 * HEAD_DIM
    qb, kb, vb = (i * DIFF_HEADS for i in range(3))
    tok = lambda off: pl.BlockSpec((seq, w), lambda b, h: (b, off + h))
    meta = lambda off: pl.BlockSpec((META_ROWS, w), lambda b, h: (0, off + h))
    vec = lambda n: pl.BlockSpec((1, n), lambda b, h: (0, 0))
    return pl.pallas_call(
        functools.partial(_diff_kernel, tq=tq, lambda_init=lambda_init),
        grid=(nb, DIFF_HEADS),
        in_specs=[tok(qb), tok(kb), tok(vb), meta(kb), meta(vb),
                  pl.BlockSpec((None, 1, LANES), lambda b, h: (h, 0, 0)),
                  vec(HEAD_DIM), vec(HEAD_DIM), vec(HEAD_DIM), vec(HEAD_DIM), vec(w)],
        out_specs=pl.BlockSpec((seq, w), lambda b, h: (b, h)),
        out_shape=jax.ShapeDtypeStruct((m, DIFF_HEADS * w), _BF16),
        scratch_shapes=[pltpu.VMEM((seq, LANES), _F32), pltpu.VMEM((2, 2, 2, tq, tq), _F32)],
        compiler_params=_params("parallel", "arbitrary"),
        name="diff_attention",
    )(proj, proj, proj, proj_meta, proj_meta, slopes, lq1, lk1, lq2, lk2, subln_g)


def _ffn(h, xn, w_gate, w_up, w_down, tm_down):
    hid = _gateup(xn, w_gate, w_up)
    return _mm(hid, w_down, _F32, tm=tm_down, tn=256)


def kernel(x, meta_tokens, ff1_pre_g, ff1_w_gate, ff1_w_up, ff1_w_down, ff1_post_g, mix_pre_g, w_in, b_forget, lambda_q1, lambda_k1, lambda_q2, lambda_k2, diff_subln_g, w_o_fox, w_o_diff, w_out, mix_post_g, ff2_pre_g, ff2_w_gate, ff2_w_up, ff2_w_down, ff2_post_g):
    nb, seq, d_model = x.shape
    depth = w_in.shape[0]
    assert depth == 1, "deeper stacks need the meta-token query path"
    h = x.reshape(nb * seq, d_model)
    hm = jnp.pad(meta_tokens.astype(x.dtype), ((0, META_ROWS - N_META), (0, 0)))
    slopes = jnp.exp2(-ALIBI_MAX_BIAS * jnp.arange(1, DIFF_HEADS + 1, dtype=_F32) / DIFF_HEADS)
    slopes = jnp.broadcast_to(slopes[:, None, None], (DIFF_HEADS, 1, LANES))
    nq = seq // ATT_TILE
    fox_w = FOX_HEADS * HEAD_DIM
    diff_w = DIFF_HEADS * 2 * HEAD_DIM
    f_col = 3 * fox_w
    vec = lambda a: a.reshape(1, -1)
    bf = lambda a: a.astype(_BF16)

    for layer in range(depth):
        lambda_init = 0.8 - 0.6 * math.exp(-0.3 * layer)

        wg, wu, wd = ff1_w_gate[layer], ff1_w_up[layer], bf(ff1_w_down[layer])
        pre, post, mix_pre = vec(ff1_pre_g[layer]), vec(ff1_post_g[layer]), vec(mix_pre_g[layer])
        y = _ffn(h, _norm(h, pre), wg, wu, wd, tm_down=512)
        h, u = _resid_norm(h, y, post, mix_pre, 0.5)
        ym = _ffn(hm, _norm(hm, pre), wg, wu, wd, tm_down=META_ROWS)
        hm, um = _resid_norm(hm, ym, post, mix_pre, 0.5)

        w = w_in[layer].T
        d_col = f_col + FOX_HEADS
        proj_f = _proj(u, w, 0, 3 * fox_w, fox_w)
        proj_d = _proj(u, w, d_col, 3 * diff_w, diff_w)
        gates = _proj(u, w, d_col + 3 * diff_w, 2 * d_model, None)
        proj_fm = _proj(um, w, 0, 3 * fox_w, fox_w)
        proj_dm = _proj(um, w, d_col, 3 * diff_w, diff_w)

        b_f = jnp.pad(b_forget[layer].astype(_F32), (0, LANES - FOX_HEADS)).reshape(1, LANES)
        cm, _ = _forget_cumsum(um, w, f_col, b_f, jnp.zeros((1, LANES), _F32), 1, N_META, META_ROWS)
        c, ct = _forget_cumsum(u, w, f_col, b_f, cm[META_ROWS - 1:], nb, CUMSUM_TILE, CUMSUM_TILE)
        ct = ct.reshape(FOX_HEADS, nb * nq, 1, ATT_TILE)

        fox = _fox_attention(proj_f, proj_fm, c, cm, ct, nb, seq)
        diff = _diff_attention(proj_d, proj_dm, slopes, vec(lambda_q1[layer]).astype(_F32),
                               vec(lambda_k1[layer]).astype(_F32), vec(lambda_q2[layer]).astype(_F32),
                               vec(lambda_k2[layer]).astype(_F32), vec(diff_subln_g[layer]).astype(_F32),
                               lambda_init, nb, seq)
        merged = _merge(fox, diff, w_o_fox[layer], w_o_diff[layer], gates, 0, d_model)
        mix = _mm(merged, w_out[layer], _F32, tm=1024, tn=512)

        wg, wu, wd = ff2_w_gate[layer], ff2_w_up[layer], bf(ff2_w_down[layer])
        pre2, post2 = vec(ff2_pre_g[layer]), vec(ff2_post_g[layer])
        h, xn = _resid_norm(h, mix, vec(mix_post_g[layer]), pre2, 1.0)
        y = _ffn(h, xn, wg, wu, wd, tm_down=512)
        h = _resid(h, y, post2, 0.5)

    return h.reshape(nb, seq, d_model)
```

```python
import functools
import math

import jax
import jax.numpy as jnp
from jax import lax
from jax.experimental import pallas as pl
from jax.experimental.pallas import tpu as pltpu

N_META = 16
FOX_HEADS = 16
DIFF_HEADS = 8
HEAD_DIM = 128
RMS_EPS = 1e-6
ALIBI_MAX_BIAS = 8.0

LANES = 128
F32_SUBLANES = 8
META_ROWS = 128
VMEM_LIMIT = 56 * 1024 * 1024
ATT_TILE = 512
FOX_HEADS_PER_STEP = 2
LOG2E = math.log2(math.e)
CUMSUM_TILE = 256

_F32 = jnp.float32
_BF16 = jnp.bfloat16
_NEG_INF = float("-inf")


def _params(*sem):
    return pltpu.CompilerParams(dimension_semantics=sem, vmem_limit_bytes=VMEM_LIMIT)


def _row_tile(m, want):
    return want if m % want == 0 else m


def _rms(x, g):
    return x * lax.rsqrt(jnp.mean(x * x, axis=-1, keepdims=True) + RMS_EPS) * g


def _sigmoid(x):
    return 1.0 / (1.0 + jnp.exp(-x))


def _dot(a, b):
    return jnp.dot(a, b, preferred_element_type=_F32)


def _dot_nt(a, b):
    return lax.dot_general(a, b, (((1,), (1,)), ((), ())), preferred_element_type=_F32)


def _norm_kernel(h_ref, g_ref, o_ref):
    o_ref[...] = _rms(h_ref[...], g_ref[...]).astype(o_ref.dtype)


def _norm(h, g, tm=256):
    m, d = h.shape
    tm = _row_tile(m, tm)
    return pl.pallas_call(
        _norm_kernel,
        grid=(m // tm,),
        in_specs=[pl.BlockSpec((tm, d), lambda i: (i, 0)),
                  pl.BlockSpec((1, d), lambda i: (0, 0))],
        out_specs=pl.BlockSpec((tm, d), lambda i: (i, 0)),
        out_shape=jax.ShapeDtypeStruct((m, d), _BF16),
        compiler_params=_params("parallel"),
        name="norm",
    )(h, g)


def _resid_norm_kernel(h_ref, y_ref, gpost_ref, gpre_ref, hn_ref, xn_ref, *, alpha):
    hn = h_ref[...] + alpha * _rms(y_ref[...], gpost_ref[...])
    hn_ref[...] = hn
    xn_ref[...] = _rms(hn, gpre_ref[...]).astype(xn_ref.dtype)


def _resid_norm(h, y, gpost, gpre, alpha, tm=256):
    m, d = h.shape
    tm = _row_tile(m, tm)
    row = pl.BlockSpec((tm, d), lambda i: (i, 0))
    vec = pl.BlockSpec((1, d), lambda i: (0, 0))
    return pl.pallas_call(
        functools.partial(_resid_norm_kernel, alpha=alpha),
        grid=(m // tm,),
        in_specs=[row, row, vec, vec],
        out_specs=[row, row],
        out_shape=[jax.ShapeDtypeStruct((m, d), _F32), jax.ShapeDtypeStruct((m, d), _BF16)],
        compiler_params=_params("parallel"),
        name="resid_norm",
    )(h, y, gpost, gpre)


def _resid_kernel(h_ref, y_ref, gpost_ref, o_ref, *, alpha):
    o_ref[...] = h_ref[...] + alpha * _rms(y_ref[...], gpost_ref[...])


def _resid(h, y, gpost, alpha, tm=256):
    m, d = h.shape
    tm = _row_tile(m, tm)
    row = pl.BlockSpec((tm, d), lambda i: (i, 0))
    vec = pl.BlockSpec((1, d), lambda i: (0, 0))
    return pl.pallas_call(
        functools.partial(_resid_kernel, alpha=alpha),
        grid=(m // tm,),
        in_specs=[row, row, vec],
        out_specs=row,
        out_shape=jax.ShapeDtypeStruct((m, d), _F32),
        compiler_params=_params("parallel"),
        name="resid",
    )(h, y, gpost)


def _mm_kernel(a_ref, w_ref, o_ref):
    o_ref[...] = _dot(a_ref[...], w_ref[...].astype(a_ref.dtype)).astype(o_ref.dtype)


def _mm(a, w, out_dtype, tm, tn):
    m, k = a.shape
    n = w.shape[1]
    tm = _row_tile(m, tm)
    return pl.pallas_call(
        _mm_kernel,
        grid=(m // tm, n // tn),
        in_specs=[pl.BlockSpec((tm, k), lambda i, j: (i, 0)),
                  pl.BlockSpec((k, tn), lambda i, j: (0, j))],
        out_specs=pl.BlockSpec((tm, tn), lambda i, j: (i, j)),
        out_shape=jax.ShapeDtypeStruct((m, n), out_dtype),
        compiler_params=_params("parallel", "arbitrary"),
        name="mm",
    )(a, w)


def _gateup_kernel(x_ref, wg_ref, wu_ref, o_ref):
    x = x_ref[...]
    g = _dot(x, wg_ref[...].astype(x.dtype))
    u = _dot(x, wu_ref[...].astype(x.dtype))
    o_ref[...] = (g * _sigmoid(g) * u).astype(o_ref.dtype)


def _gateup(x, wg, wu, tm=2048, tn=256):
    m, k = x.shape
    n = wg.shape[1]
    tm = _row_tile(m, tm)
    wspec = pl.BlockSpec((k, tn), lambda i, j: (0, j))
    return pl.pallas_call(
        _gateup_kernel,
        grid=(m // tm, n // tn),
        in_specs=[pl.BlockSpec((tm, k), lambda i, j: (i, 0), pipeline_mode=pl.Buffered(1)),
                  wspec, wspec],
        out_specs=pl.BlockSpec((tm, tn), lambda i, j: (i, j)),
        out_shape=jax.ShapeDtypeStruct((m, n), _BF16),
        compiler_params=_params("parallel", "arbitrary"),
        name="gateup",
    )(x, wg, wu)


def _proj_kernel(u_ref, wt_ref, o_ref, *, tn, q_cols, q_scale):
    acc = _dot_nt(u_ref[...], wt_ref[...].astype(u_ref.dtype))
    if q_cols is None:
        o_ref[...] = _sigmoid(acc).astype(o_ref.dtype)
    else:
        scale = jnp.where(pl.program_id(1) * tn < q_cols, q_scale, 1.0).astype(_F32)
        o_ref[...] = (acc * scale).astype(o_ref.dtype)


def _proj(u, w_in_t, col0, n_out, q_cols, tm=1024, tn=512):
    m, k = u.shape
    tm = _row_tile(m, tm)
    return pl.pallas_call(
        functools.partial(_proj_kernel, tn=tn, q_cols=q_cols, q_scale=HEAD_DIM ** -0.5 * LOG2E),
        grid=(m // tm, n_out // tn),
        in_specs=[pl.BlockSpec((tm, k), lambda i, j: (i, 0)),
                  pl.BlockSpec((pl.Element(tn), pl.Element(k)),
                               lambda i, j: (pl.multiple_of(col0 + j * tn, F32_SUBLANES), 0))],
        out_specs=pl.BlockSpec((tm, tn), lambda i, j: (i, j)),
        out_shape=jax.ShapeDtypeStruct((m, n_out), _BF16),
        compiler_params=_params("parallel", "arbitrary"),
        name="proj_gate" if q_cols is None else "proj_qkv",
    )(u, w_in_t)


def _merge_kernel(af_ref, ad_ref, wf_ref, wd_ref, gf_ref, gd_ref, o_ref):
    yf = _dot(af_ref[...], wf_ref[...].astype(af_ref.dtype))
    yd = _dot(ad_ref[...], wd_ref[...].astype(ad_ref.dtype))
    o_ref[...] = (gf_ref[...].astype(_F32) * yf + gd_ref[...].astype(_F32) * yd).astype(o_ref.dtype)


def _merge(fox, diff, w_of, w_od, gates, gf_col, gd_col, tm=1024, tn=512):
    m, k = fox.shape
    n = w_of.shape[1]
    aspec = pl.BlockSpec((tm, k), lambda i, j: (i, 0))
    wspec = pl.BlockSpec((k, tn), lambda i, j: (0, j))
    return pl.pallas_call(
        _merge_kernel,
        grid=(m // tm, n // tn),
        in_specs=[aspec, aspec, wspec, wspec,
                  pl.BlockSpec((tm, tn), lambda i, j: (i, gf_col // tn + j)),
                  pl.BlockSpec((tm, tn), lambda i, j: (i, gd_col // tn + j))],
        out_specs=pl.BlockSpec((tm, tn), lambda i, j: (i, j)),
        out_shape=jax.ShapeDtypeStruct((m, n), _BF16),
        compiler_params=_params("parallel", "arbitrary"),
        name="merge",
    )(fox, diff, w_of, w_od, gates, gates)


def _cumsum_kernel(u_ref, wf_ref, bf_ref, init_ref, c_ref, ct_ref, carry_ref, *, n_valid):
    tm = u_ref.shape[0]

    @pl.when(pl.program_id(1) == 0)
    def _():
        carry_ref[...] = init_ref[...]

    is_head = lax.broadcasted_iota(jnp.int32, wf_ref.shape, 0) < FOX_HEADS
    wf = jnp.where(is_head, wf_ref[...], 0.0).astype(u_ref.dtype)
    x = _dot_nt(u_ref[...], wf) + bf_ref[...]
    logf = jnp.minimum(x, 0.0) - jnp.log1p(jnp.exp(-jnp.abs(x)))
    row = lax.broadcasted_iota(jnp.int32, (tm, tm), 0)
    col = lax.broadcasted_iota(jnp.int32, (tm, tm), 1)
    if n_valid < tm:
        logf = jnp.where(lax.broadcasted_iota(jnp.int32, (tm, LANES), 0) < n_valid, logf, 0.0)
    tri = (row >= col).astype(_F32)
    c = jnp.dot(tri, logf, precision=lax.Precision.HIGHEST, preferred_element_type=_F32)
    c = c + carry_ref[...]
    c_ref[...] = c
    ct_ref[...] = c.T[:FOX_HEADS, :]
    carry_ref[...] = c[tm - 1:tm, :]


def _forget_cumsum(u, w_in_t, f_col, bf, init, nb, n_valid, tm):
    m, k = u.shape
    nt = m // (nb * tm)
    f_blk = f_col // LANES
    assert f_blk * LANES == f_col
    return pl.pallas_call(
        functools.partial(_cumsum_kernel, n_valid=n_valid),
        grid=(nb, nt),
        in_specs=[pl.BlockSpec((tm, k), lambda b, t: (b * nt + t, 0)),
                  pl.BlockSpec((LANES, k), lambda b, t: (f_blk, 0)),
                  pl.BlockSpec((1, LANES), lambda b, t: (0, 0)),
                  pl.BlockSpec((1, LANES), lambda b, t: (0, 0))],
        out_specs=[pl.BlockSpec((tm, LANES), lambda b, t: (b * nt + t, 0)),
                   pl.BlockSpec((FOX_HEADS, tm), lambda b, t: (0, b * nt + t))],
        out_shape=[jax.ShapeDtypeStruct((m, LANES), _F32),
                   jax.ShapeDtypeStruct((FOX_HEADS, m), _F32)],
        scratch_shapes=[pltpu.VMEM((1, LANES), _F32)],
        compiler_params=_params("arbitrary", "arbitrary"),
        name="forget_cumsum",
    )(u, w_in_t, bf, init)


def _dot_tn(a, b):
    return lax.dot_general(a, b, (((0,), (0,)), ((), ())), preferred_element_type=_F32)


def _lane_tile(x, n):
    return jnp.concatenate([x] * (n // LANES), axis=1)


def _flash_step(t, v, rq, m, l, acc):
    m_new = jnp.maximum(m, jnp.max(t, axis=0, keepdims=True) + rq)
    alpha = jnp.exp2(m - m_new)
    p = jnp.exp2(t - (m_new - rq))
    l = alpha * l + jnp.sum(p, axis=0, keepdims=True)
    acc = alpha * acc + _dot_tn(v, p.astype(v.dtype))
    return m_new, l, acc


def _flash_init(dv, tq):
    return (jnp.full((1, tq), _NEG_INF, _F32), jnp.zeros((1, tq), _F32), jnp.zeros((dv, tq), _F32))


class _QueryTile:
    def __init__(self, streams, dv, tq, rq, first, scores, values, t_ref):
        self.streams, self.dv, self.tq, self.rq = streams, dv, tq, rq
        self.first, self.scores, self.values, self.t_ref = first, scores, values, t_ref

    def start(self):
        for s in range(self.streams):
            self.t_ref[0, s] = self.scores(s, 0)
        state = []
        for s in range(self.streams):
            state.extend(_flash_step(*self.first(s), self.rq[s], *_flash_init(self.dv, self.tq)))
        return tuple(state)

    def run_pairs(self, n_pairs, state):
        def half(st, slot, blk, nxt):
            for s in range(self.streams):
                self.t_ref[1 - slot, s] = self.scores(s, nxt)
            out = []
            for s in range(self.streams):
                out.extend(_flash_step(self.t_ref[slot, s], self.values(s, blk), self.rq[s],
                                       *st[3 * s:3 * s + 3]))
            return tuple(out)

        def pair(j, st):
            st = half(st, 0, 2 * j, 2 * j + 1)
            return half(st, 1, 2 * j + 1, jnp.minimum(2 * j + 2, 2 * n_pairs - 1))

        return lax.fori_loop(0, n_pairs, pair, state)


def _sweep_tile_pairs(n_tiles, make_tile, finish):
    def body(a, carry):
        tiles = [make_tile(2 * a, 0, None), make_tile(2 * a + 1, 1, 2 * a)]
        states = [t.start() for t in tiles]
        states = [t.run_pairs(a, st) for t, st in zip(tiles, states)]
        for t, st in zip(tiles, states):
            finish(t, st)
        return carry

    lax.fori_loop(0, n_tiles // 2, body, 0)


def _fox_kernel(q_ref, k_ref, v_ref, km_ref, vm_ref, c_ref, cm_ref, ct_ref, o_ref, colb_ref,
                colbm_ref, t_ref, *, tq, hps):
    nq = q_ref.shape[0] // tq
    d = HEAD_DIM
    heads = [pl.program_id(1) * hps + hh for hh in range(hps)]
    lanes = lambda hh: slice(hh * d, (hh + 1) * d)

    lane = lax.broadcasted_iota(jnp.int32, (tq, LANES), 1)
    lane_m = lax.broadcasted_iota(jnp.int32, (META_ROWS, LANES), 1)
    for hh in range(hps):
        def fill(i, carry, hh=hh):
            rs = pl.multiple_of(i * tq, tq)
            ck = jnp.sum(jnp.where(lane == heads[hh], c_ref[pl.ds(rs, tq), :], 0.0),
                         axis=1, keepdims=True)
            colb_ref[hh, pl.ds(rs, tq), :] = jnp.broadcast_to(-LOG2E * ck, (tq, LANES))
            return carry

        lax.fori_loop(0, nq, fill, 0)
        ckm = jnp.sum(jnp.where(lane_m == heads[hh], cm_ref[...], 0.0), axis=1, keepdims=True)
        colbm_ref[hh] = jnp.broadcast_to(-LOG2E * ckm, (META_ROWS, LANES))

    meta_valid = lax.broadcasted_iota(jnp.int32, (META_ROWS, tq), 0) < N_META
    causal = (lax.broadcasted_iota(jnp.int32, (tq, tq), 1)
              >= lax.broadcasted_iota(jnp.int32, (tq, tq), 0))

    def make_tile(qi, slot, odd_block):
        qs = pl.multiple_of(qi * tq, tq)
        q = [q_ref[pl.ds(qs, tq), lanes(hh)] for hh in range(hps)]
        rq = [LOG2E * ct_ref[hh, qi] for hh in range(hps)]

        def scores(hh, j):
            ks = pl.multiple_of(j * tq, tq)
            return (_dot_nt(k_ref[pl.ds(ks, tq), lanes(hh)], q[hh])
                    + _lane_tile(colb_ref[hh, pl.ds(ks, tq), :], tq))

        def values(hh, j):
            return v_ref[pl.ds(pl.multiple_of(j * tq, tq), tq), lanes(hh)]

        def first(hh):
            tm = _dot_nt(km_ref[:, lanes(hh)], q[hh]) + _lane_tile(colbm_ref[hh], tq)
            ts = [jnp.where(causal, scores(hh, qi), _NEG_INF), jnp.where(meta_valid, tm, _NEG_INF)]
            vs = [values(hh, qi), vm_ref[:, lanes(hh)]]
            if odd_block is not None:
                ts.append(scores(hh, odd_block))
                vs.append(values(hh, odd_block))
            return jnp.concatenate(ts, axis=0), jnp.concatenate(vs, axis=0)

        tile = _QueryTile(hps, d, tq, rq, first, scores, values, t_ref.at[slot])
        tile.qs = qs
        return tile

    def finish(tile, state):
        for hh in range(hps):
            _, l, acc = state[3 * hh:3 * hh + 3]
            o_ref[pl.ds(tile.qs, tq), lanes(hh)] = (acc / l).T.astype(o_ref.dtype)

    _sweep_tile_pairs(nq, make_tile, finish)


def _fox_attention(proj, proj_meta, c, cm, ct, nb, seq, tq=ATT_TILE, hps=FOX_HEADS_PER_STEP):
    m = proj.shape[0]
    w = hps * HEAD_DIM
    qb, kb, vb = (i * FOX_HEADS // hps for i in range(3))
    nq = seq // tq
    tok = lambda off: pl.BlockSpec((seq, w), lambda b, s: (b, off + s))
    meta = lambda off: pl.BlockSpec((META_ROWS, w), lambda b, s: (0, off + s))
    return pl.pallas_call(
        functools.partial(_fox_kernel, tq=tq, hps=hps),
        grid=(nb, FOX_HEADS // hps),
        in_specs=[tok(qb), tok(kb), tok(vb), meta(kb), meta(vb),
                  pl.BlockSpec((seq, LANES), lambda b, s: (b, 0)),
                  pl.BlockSpec((META_ROWS, LANES), lambda b, s: (0, 0)),
                  pl.BlockSpec((hps, nq, 1, tq), lambda b, s: (s, b, 0, 0))],
        out_specs=pl.BlockSpec((seq, w), lambda b, s: (b, s)),
        out_shape=jax.ShapeDtypeStruct((m, FOX_HEADS * HEAD_DIM), _BF16),
        scratch_shapes=[pltpu.VMEM((hps, seq, LANES), _F32),
                        pltpu.VMEM((hps, META_ROWS, LANES), _F32),
                        pltpu.VMEM((2, 2, hps, tq, tq), _F32)],
        compiler_params=_params("parallel", "arbitrary"),
        name="fox_attention",
    )(proj, proj, proj, proj_meta, proj_meta, c, cm, ct)


def _diff_kernel(q_ref, k_ref, v_ref, km_ref, vm_ref, slope_ref, lq1_ref, lk1_ref, lq2_ref,
                 lk2_ref, g_ref, o_ref, colb_ref, t_ref, *, tq, lambda_init):
    nq = q_ref.shape[0] // tq
    d = HEAD_DIM
    comp = lambda c: slice(c * d, (c + 1) * d)
    slope2 = LOG2E * slope_ref[:, 0:1]

    key_row = lax.broadcasted_iota(jnp.int32, (tq, LANES), 0)

    def fill(i, carry):
        rs = pl.multiple_of(i * tq, tq)
        colb_ref[pl.ds(rs, tq), :] = slope2 * (key_row + (rs + N_META)).astype(_F32)
        return carry

    lax.fori_loop(0, nq, fill, 0)
    colbm = slope2 * lax.broadcasted_iota(jnp.int32, (META_ROWS, LANES), 0).astype(_F32)
    query_col = lax.broadcasted_iota(jnp.int32, (1, tq), 1)

    meta_valid = lax.broadcasted_iota(jnp.int32, (META_ROWS, tq), 0) < N_META
    causal = (lax.broadcasted_iota(jnp.int32, (tq, tq), 1)
              >= lax.broadcasted_iota(jnp.int32, (tq, tq), 0))
    lam = (jnp.exp(jnp.sum(lq1_ref[...] * lk1_ref[...], axis=1, keepdims=True))
           - jnp.exp(jnp.sum(lq2_ref[...] * lk2_ref[...], axis=1, keepdims=True)) + lambda_init)

    def make_tile(qi, slot, odd_block):
        qs = pl.multiple_of(qi * tq, tq)
        q = [q_ref[pl.ds(qs, tq), comp(c)] for c in range(2)]
        rq = -slope2 * (query_col + (qs + N_META)).astype(_F32)

        def scores(c, j):
            ks = pl.multiple_of(j * tq, tq)
            return (_dot_nt(k_ref[pl.ds(ks, tq), comp(c)], q[c])
                    + _lane_tile(colb_ref[pl.ds(ks, tq), :], tq))

        def values(c, j):
            return v_ref[pl.ds(pl.multiple_of(j * tq, tq), tq), :]

        def first(c):
            tm = _dot_nt(km_ref[:, comp(c)], q[c]) + _lane_tile(colbm, tq)
            ts = [jnp.where(causal, scores(c, qi), _NEG_INF), jnp.where(meta_valid, tm, _NEG_INF)]
            vs = [values(c, qi), vm_ref[...]]
            if odd_block is not None:
                ts.append(scores(c, odd_block))
                vs.append(values(c, odd_block))
            return jnp.concatenate(ts, axis=0), jnp.concatenate(vs, axis=0)

        tile = _QueryTile(2, 2 * d, tq, (rq, rq), first, scores, values, t_ref.at[slot])
        tile.qs = qs
        return tile

    def finish(tile, state):
        _, l1, a1, _, l2, a2 = state
        o = (a1 / l1 - lam * (a2 / l2)).T
        y = _rms(o, g_ref[...]) * (1.0 - lambda_init)
        o_ref[pl.ds(tile.qs, tq), :] = y.astype(o_ref.dtype)

    _sweep_tile_pairs(nq, make_tile, finish)


def _diff_attention(proj, proj_meta, slopes, lq1, lk1, lq2, lk2, subln_g, lambda_init, nb, seq,
                    tq=ATT_TILE):
    m = proj.shape[0]
    w = 2 * HEAD_DIM
    qb, kb, vb = (i * DIFF_HEADS for i in range(3))
    tok = lambda off: pl.BlockSpec((seq, w), lambda b, h: (b, off + h))
    meta = lambda off: pl.BlockSpec((META_ROWS, w), lambda b, h: (0, off + h))
    vec = lambda n: pl.BlockSpec((1, n), lambda b, h: (0, 0))
    return pl.pallas_call(
        functools.partial(_diff_kernel, tq=tq, lambda_init=lambda_init),
        grid=(nb, DIFF_HEADS),
        in_specs=[tok(qb), tok(kb), tok(vb), meta(kb), meta(vb),
                  pl.BlockSpec((None, 1, LANES), lambda b, h: (h, 0, 0)),
                  vec(HEAD_DIM), vec(HEAD_DIM), vec(HEAD_DIM), vec(HEAD_DIM), vec(w)],
        out_specs=pl.BlockSpec((seq, w), lambda b, h: (b, h)),
        out_shape=jax.ShapeDtypeStruct((m, DIFF_HEADS * w), _BF16),
        scratch_shapes=[pltpu.VMEM((seq, LANES), _F32), pltpu.VMEM((2, 2, 2, tq, tq), _F32)],
        compiler_params=_params("parallel", "arbitrary"),
        name="diff_attention",
    )(proj, proj, proj, proj_meta, proj_meta, slopes, lq1, lk1, lq2, lk2, subln_g)


def _ffn(h, xn, w_gate, w_up, w_down, tm_down):
    hid = _gateup(xn, w_gate, w_up)
    return _mm(hid, w_down, _F32, tm=tm_down, tn=512)


def kernel(x, meta_tokens, ff1_pre_g, ff1_w_gate, ff1_w_up, ff1_w_down, ff1_post_g, mix_pre_g, w_in, b_forget, lambda_q1, lambda_k1, lambda_q2, lambda_k2, diff_subln_g, w_o_fox, w_o_diff, w_out, mix_post_g, ff2_pre_g, ff2_w_gate, ff2_w_up, ff2_w_down, ff2_post_g):
    nb, seq, d_model = x.shape
    depth = w_in.shape[0]
    assert depth == 1, "deeper stacks need the meta-token query path"
    h = x.reshape(nb * seq, d_model)
    hm = jnp.pad(meta_tokens.astype(x.dtype), ((0, META_ROWS - N_META), (0, 0)))
    slopes = jnp.exp2(-ALIBI_MAX_BIAS * jnp.arange(1, DIFF_HEADS + 1, dtype=_F32) / DIFF_HEADS)
    slopes = jnp.broadcast_to(slopes[:, None, None], (DIFF_HEADS, 1, LANES))
    nq = seq // ATT_TILE
    fox_w = FOX_HEADS * HEAD_DIM
    diff_w = DIFF_HEADS * 2 * HEAD_DIM
    f_col = 3 * fox_w
    vec = lambda a: a.reshape(1, -1)
    bf = lambda a: a.astype(_BF16)

    for layer in range(depth):
        lambda_init = 0.8 - 0.6 * math.exp(-0.3 * layer)

        wg, wu, wd = ff1_w_gate[layer], ff1_w_up[layer], bf(ff1_w_down[layer])
        pre, post, mix_pre = vec(ff1_pre_g[layer]), vec(ff1_post_g[layer]), vec(mix_pre_g[layer])
        y = _ffn(h, _norm(h, pre), wg, wu, wd, tm_down=512)
        h, u = _resid_norm(h, y, post, mix_pre, 0.5)
        ym = _ffn(hm, _norm(hm, pre), wg, wu, wd, tm_down=META_ROWS)
        hm, um = _resid_norm(hm, ym, post, mix_pre, 0.5)

        w = w_in[layer].T
        d_col = f_col + FOX_HEADS
        proj_f = _proj(u, w, 0, 3 * fox_w, fox_w)
        proj_d = _proj(u, w, d_col, 3 * diff_w, diff_w)
        gates = _proj(u, w, d_col + 3 * diff_w, 2 * d_model, None)
        proj_fm = _proj(um, w, 0, 3 * fox_w, fox_w)
        proj_dm = _proj(um, w, d_col, 3 * diff_w, diff_w)

        b_f = jnp.pad(b_forget[layer].astype(_F32), (0, LANES - FOX_HEADS)).reshape(1, LANES)
        cm, _ = _forget_cumsum(um, w, f_col, b_f, jnp.zeros((1, LANES), _F32), 1, N_META, META_ROWS)
        c, ct = _forget_cumsum(u, w, f_col, b_f, cm[META_ROWS - 1:], nb, CUMSUM_TILE, CUMSUM_TILE)
        ct = ct.reshape(FOX_HEADS, nb * nq, 1, ATT_TILE)

        fox = _fox_attention(proj_f, proj_fm, c, cm, ct, nb, seq)
        diff = _diff_attention(proj_d, proj_dm, slopes, vec(lambda_q1[layer]).astype(_F32),
                               vec(lambda_k1[layer]).astype(_F32), vec(lambda_q2[layer]).astype(_F32),
                               vec(lambda_k2[layer]).astype(_F32), vec(diff_subln_g[layer]).astype(_F32),
                               lambda_init, nb, seq)
        merged = _merge(fox, diff, w_o_fox[layer], w_o_diff[layer], gates, 0, d_model)
        mix = _mm(merged, w_out[layer], _F32, tm=1024, tn=512)

        wg, wu, wd = ff2_w_gate[layer], ff2_w_up[layer], bf(ff2_w_down[layer])
        pre2, post2 = vec(ff2_pre_g[layer]), vec(ff2_post_g[layer])
        h, xn = _resid_norm(h, mix, vec(mix_post_g[layer]), pre2, 1.0)
        y = _ffn(h, xn, wg, wu, wd, tm_down=512)
        h = _resid(h, y, post2, 0.5)

    return h.reshape(nb, seq, d_model)
```

```python
import functools
import math

import jax
import jax.numpy as jnp
from jax import lax
from jax.experimental import pallas as pl
from jax.experimental.pallas import tpu as pltpu

N_META = 16
FOX_HEADS = 16
DIFF_HEADS = 8
HEAD_DIM = 128
RMS_EPS = 1e-6
ALIBI_MAX_BIAS = 8.0

LANES = 128
F32_SUBLANES = 8
META_ROWS = 128
VMEM_LIMIT = 56 * 1024 * 1024
ATT_TILE = 512
FOX_HEADS_PER_STEP = 2
LOG2E = math.log2(math.e)
CUMSUM_TILE = 256

_F32 = jnp.float32
_BF16 = jnp.bfloat16
_NEG_INF = float("-inf")
SUBLAYER_OUT_DTYPE = _BF16


def _params(*sem):
    return pltpu.CompilerParams(dimension_semantics=sem, vmem_limit_bytes=VMEM_LIMIT)


def _row_tile(m, want):
    return want if m % want == 0 else m


def _rms(x, g):
    return x * lax.rsqrt(jnp.mean(x * x, axis=-1, keepdims=True) + RMS_EPS) * g


def _sigmoid(x):
    return 0.5 * jnp.tanh(0.5 * x) + 0.5


def _dot(a, b):
    return jnp.dot(a, b, preferred_element_type=_F32)


def _dot_nt(a, b):
    return lax.dot_general(a, b, (((1,), (1,)), ((), ())), preferred_element_type=_F32)


def _norm_kernel(h_ref, g_ref, o_ref):
    o_ref[...] = _rms(h_ref[...], g_ref[...]).astype(o_ref.dtype)


def _norm(h, g, tm=256):
    m, d = h.shape
    tm = _row_tile(m, tm)
    return pl.pallas_call(
        _norm_kernel,
        grid=(m // tm,),
        in_specs=[pl.BlockSpec((tm, d), lambda i: (i, 0)),
                  pl.BlockSpec((1, d), lambda i: (0, 0))],
        out_specs=pl.BlockSpec((tm, d), lambda i: (i, 0)),
        out_shape=jax.ShapeDtypeStruct((m, d), _BF16),
        compiler_params=_params("parallel"),
        name="norm",
    )(h, g)


def _resid_norm_kernel(h_ref, y_ref, gpost_ref, gpre_ref, hn_ref, xn_ref, *, alpha):
    hn = h_ref[...] + alpha * _rms(y_ref[...].astype(_F32), gpost_ref[...])
    hn_ref[...] = hn
    xn_ref[...] = _rms(hn, gpre_ref[...]).astype(xn_ref.dtype)


def _resid_norm(h, y, gpost, gpre, alpha, tm=256):
    m, d = h.shape
    tm = _row_tile(m, tm)
    row = pl.BlockSpec((tm, d), lambda i: (i, 0))
    vec = pl.BlockSpec((1, d), lambda i: (0, 0))
    return pl.pallas_call(
        functools.partial(_resid_norm_kernel, alpha=alpha),
        grid=(m // tm,),
        in_specs=[row, row, vec, vec],
        out_specs=[row, row],
        out_shape=[jax.ShapeDtypeStruct((m, d), _F32), jax.ShapeDtypeStruct((m, d), _BF16)],
        compiler_params=_params("parallel"),
        name="resid_norm",
    )(h, y, gpost, gpre)


def _resid_kernel(h_ref, y_ref, gpost_ref, o_ref, *, alpha):
    o_ref[...] = h_ref[...] + alpha * _rms(y_ref[...].astype(_F32), gpost_ref[...])


def _resid(h, y, gpost, alpha, tm=256):
    m, d = h.shape
    tm = _row_tile(m, tm)
    row = pl.BlockSpec((tm, d), lambda i: (i, 0))
    vec = pl.BlockSpec((1, d), lambda i: (0, 0))
    return pl.pallas_call(
        functools.partial(_resid_kernel, alpha=alpha),
        grid=(m // tm,),
        in_specs=[row, row, vec],
        out_specs=row,
        out_shape=jax.ShapeDtypeStruct((m, d), _F32),
        compiler_params=_params("parallel"),
        name="resid",
    )(h, y, gpost)


def _mm_kernel(a_ref, w_ref, o_ref):
    o_ref[...] = _dot(a_ref[...], w_ref[...].astype(a_ref.dtype)).astype(o_ref.dtype)


def _mm(a, w, out_dtype, tm, tn):
    m, k = a.shape
    n = w.shape[1]
    tm = _row_tile(m, tm)
    return pl.pallas_call(
        _mm_kernel,
        grid=(m // tm, n // tn),
        in_specs=[pl.BlockSpec((tm, k), lambda i, j: (i, 0)),
                  pl.BlockSpec((k, tn), lambda i, j: (0, j))],
        out_specs=pl.BlockSpec((tm, tn), lambda i, j: (i, j)),
        out_shape=jax.ShapeDtypeStruct((m, n), out_dtype),
        compiler_params=_params("parallel", "arbitrary"),
        name="mm",
    )(a, w)


def _gateup_kernel(x_ref, wg_ref, wu_ref, o_ref):
    x = x_ref[...]
    g = _dot(x, wg_ref[...].astype(x.dtype))
    u = _dot(x, wu_ref[...].astype(x.dtype))
    o_ref[...] = (g * _sigmoid(g) * u).astype(o_ref.dtype)


def _gateup(x, wg, wu, tm=2048, tn=256):
    m, k = x.shape
    n = wg.shape[1]
    tm = _row_tile(m, tm)
    wspec = pl.BlockSpec((k, tn), lambda i, j: (0, j))
    return pl.pallas_call(
        _gateup_kernel,
        grid=(m // tm, n // tn),
        in_specs=[pl.BlockSpec((tm, k), lambda i, j: (i, 0), pipeline_mode=pl.Buffered(1)),
                  wspec, wspec],
        out_specs=pl.BlockSpec((tm, tn), lambda i, j: (i, j)),
        out_shape=jax.ShapeDtypeStruct((m, n), _BF16),
        compiler_params=_params("parallel", "arbitrary"),
        name="gateup",
    )(x, wg, wu)


def _proj_kernel(u_ref, wt_ref, o_ref, *, tn, q_cols, q_scale):
    acc = _dot_nt(u_ref[...], wt_ref[...].astype(u_ref.dtype))
    if q_cols is None:
        o_ref[...] = _sigmoid(acc).astype(o_ref.dtype)
    else:
        scale = jnp.where(pl.program_id(1) * tn < q_cols, q_scale, 1.0).astype(_F32)
        o_ref[...] = (acc * scale).astype(o_ref.dtype)


def _proj(u, w_in_t, col0, n_out, q_cols, tm=1024, tn=512):
    m, k = u.shape
    tm = _row_tile(m, tm)
    return pl.pallas_call(
        functools.partial(_proj_kernel, tn=tn, q_cols=q_cols, q_scale=HEAD_DIM ** -0.5 * LOG2E),
        grid=(m // tm, n_out // tn),
        in_specs=[pl.BlockSpec((tm, k), lambda i, j: (i, 0)),
                  pl.BlockSpec((pl.Element(tn), pl.Element(k)),
                               lambda i, j: (pl.multiple_of(col0 + j * tn, F32_SUBLANES), 0))],
        out_specs=pl.BlockSpec((tm, tn), lambda i, j: (i, j)),
        out_shape=jax.ShapeDtypeStruct((m, n_out), _BF16),
        compiler_params=_params("parallel", "arbitrary"),
        name="proj_gate" if q_cols is None else "proj_qkv",
    )(u, w_in_t)


def _merge_kernel(af_ref, ad_ref, wf_ref, wd_ref, gf_ref, gd_ref, o_ref):
    yf = _dot(af_ref[...], wf_ref[...].astype(af_ref.dtype))
    yd = _dot(ad_ref[...], wd_ref[...].astype(ad_ref.dtype))
    o_ref[...] = (gf_ref[...].astype(_F32) * yf + gd_ref[...].astype(_F32) * yd).astype(o_ref.dtype)


def _merge(fox, diff, w_of, w_od, gates, gf_col, gd_col, tm=1024, tn=512):
    m, k = fox.shape
    n = w_of.shape[1]
    aspec = pl.BlockSpec((tm, k), lambda i, j: (i, 0))
    wspec = pl.BlockSpec((k, tn), lambda i, j: (0, j))
    return pl.pallas_call(
        _merge_kernel,
        grid=(m // tm, n // tn),
        in_specs=[aspec, aspec, wspec, wspec,
                  pl.BlockSpec((tm, tn), lambda i, j: (i, gf_col // tn + j)),
                  pl.BlockSpec((tm, tn), lambda i, j: (i, gd_col // tn + j))],
        out_specs=pl.BlockSpec((tm, tn), lambda i, j: (i, j)),
        out_shape=jax.ShapeDtypeStruct((m, n), _BF16),
        compiler_params=_params("parallel", "arbitrary"),
        name="merge",
    )(fox, diff, w_of, w_od, gates, gates)


def _cumsum_kernel(u_ref, wf_ref, bf_ref, init_ref, c_ref, ct_ref, carry_ref, *, n_valid):
    tm = u_ref.shape[0]

    @pl.when(pl.program_id(1) == 0)
    def _():
        carry_ref[...] = init_ref[...]

    is_head = lax.broadcasted_iota(jnp.int32, wf_ref.shape, 0) < FOX_HEADS
    wf = jnp.where(is_head, wf_ref[...], 0.0).astype(u_ref.dtype)
    x = _dot_nt(u_ref[...], wf) + bf_ref[...]
    logf = jnp.minimum(x, 0.0) - jnp.log1p(jnp.exp(-jnp.abs(x)))
    row = lax.broadcasted_iota(jnp.int32, (tm, tm), 0)
    col = lax.broadcasted_iota(jnp.int32, (tm, tm), 1)
    if n_valid < tm:
        logf = jnp.where(lax.broadcasted_iota(jnp.int32, (tm, LANES), 0) < n_valid, logf, 0.0)
    tri = (row >= col).astype(_F32)
    c = jnp.dot(tri, logf, precision=lax.Precision.HIGHEST, preferred_element_type=_F32)
    c = c + carry_ref[...]
    c_ref[...] = c
    ct_ref[...] = c.T[:FOX_HEADS, :]
    carry_ref[...] = c[tm - 1:tm, :]


def _forget_cumsum(u, w_in_t, f_col, bf, init, nb, n_valid, tm):
    m, k = u.shape
    nt = m // (nb * tm)
    f_blk = f_col // LANES
    assert f_blk * LANES == f_col
    return pl.pallas_call(
        functools.partial(_cumsum_kernel, n_valid=n_valid),
        grid=(nb, nt),
        in_specs=[pl.BlockSpec((tm, k), lambda b, t: (b * nt + t, 0)),
                  pl.BlockSpec((LANES, k), lambda b, t: (f_blk, 0)),
                  pl.BlockSpec((1, LANES), lambda b, t: (0, 0)),
                  pl.BlockSpec((1, LANES), lambda b, t: (0, 0))],
        out_specs=[pl.BlockSpec((tm, LANES), lambda b, t: (b * nt + t, 0)),
                   pl.BlockSpec((FOX_HEADS, tm), lambda b, t: (0, b * nt + t))],
        out_shape=[jax.ShapeDtypeStruct((m, LANES), _F32),
                   jax.ShapeDtypeStruct((FOX_HEADS, m), _F32)],
        scratch_shapes=[pltpu.VMEM((1, LANES), _F32)],
        compiler_params=_params("arbitrary", "arbitrary"),
        name="forget_cumsum",
    )(u, w_in_t, bf, init)


def _dot_tn(a, b):
    return lax.dot_general(a, b, (((0,), (0,)), ((), ())), preferred_element_type=_F32)


def _lane_tile(x, n):
    return jnp.concatenate([x] * (n // LANES), axis=1)


def _flash_step(t, v, rq, m, l, acc):
    m_new = jnp.maximum(m, jnp.max(t, axis=0, keepdims=True) + rq)
    alpha = jnp.exp2(m - m_new)
    p = jnp.exp2(t - (m_new - rq))
    l = alpha * l + jnp.sum(p, axis=0, keepdims=True)
    acc = alpha * acc + _dot_tn(v, p.astype(v.dtype))
    return m_new, l, acc


def _flash_init(dv, tq):
    return (jnp.full((1, tq), _NEG_INF, _F32), jnp.zeros((1, tq), _F32), jnp.zeros((dv, tq), _F32))


class _QueryTile:
    def __init__(self, streams, dv, tq, rq, first, scores, values, t_ref):
        self.streams, self.dv, self.tq, self.rq = streams, dv, tq, rq
        self.first, self.scores, self.values, self.t_ref = first, scores, values, t_ref

    def start(self):
        for s in range(self.streams):
            self.t_ref[0, s] = self.scores(s, 0)
        state = []
        for s in range(self.streams):
            state.extend(_flash_step(*self.first(s), self.rq[s], *_flash_init(self.dv, self.tq)))
        return tuple(state)

    def run_pairs(self, n_pairs, state):
        def half(st, slot, blk, nxt):
            for s in range(self.streams):
                self.t_ref[1 - slot, s] = self.scores(s, nxt)
            out = []
            for s in range(self.streams):
                out.extend(_flash_step(self.t_ref[slot, s], self.values(s, blk), self.rq[s],
                                       *st[3 * s:3 * s + 3]))
            return tuple(out)

        def pair(j, st):
            st = half(st, 0, 2 * j, 2 * j + 1)
            return half(st, 1, 2 * j + 1, jnp.minimum(2 * j + 2, 2 * n_pairs - 1))

        return lax.fori_loop(0, n_pairs, pair, state)


def _sweep_tile_pairs(n_tiles, make_tile, finish):
    def body(a, carry):
        tiles = [make_tile(2 * a, 0, None), make_tile(2 * a + 1, 1, 2 * a)]
        states = [t.start() for t in tiles]
        states = [t.run_pairs(a, st) for t, st in zip(tiles, states)]
        for t, st in zip(tiles, states):
            finish(t, st)
        return carry

    lax.fori_loop(0, n_tiles // 2, body, 0)


def _fox_kernel(q_ref, k_ref, v_ref, km_ref, vm_ref, c_ref, cm_ref, ct_ref, o_ref, colb_ref,
                colbm_ref, t_ref, *, tq, hps):
    nq = q_ref.shape[0] // tq
    d = HEAD_DIM
    heads = [pl.program_id(1) * hps + hh for hh in range(hps)]
    lanes = lambda hh: slice(hh * d, (hh + 1) * d)

    lane = lax.broadcasted_iota(jnp.int32, (tq, LANES), 1)
    lane_m = lax.broadcasted_iota(jnp.int32, (META_ROWS, LANES), 1)
    for hh in range(hps):
        def fill(i, carry, hh=hh):
            rs = pl.multiple_of(i * tq, tq)
            ck = jnp.sum(jnp.where(lane == heads[hh], c_ref[pl.ds(rs, tq), :], 0.0),
                         axis=1, keepdims=True)
            colb_ref[hh, pl.ds(rs, tq), :] = jnp.broadcast_to(-LOG2E * ck, (tq, LANES))
            return carry

        lax.fori_loop(0, nq, fill, 0)
        ckm = jnp.sum(jnp.where(lane_m == heads[hh], cm_ref[...], 0.0), axis=1, keepdims=True)
        colbm_ref[hh] = jnp.broadcast_to(-LOG2E * ckm, (META_ROWS, LANES))

    meta_valid = lax.broadcasted_iota(jnp.int32, (META_ROWS, tq), 0) < N_META
    causal = (lax.broadcasted_iota(jnp.int32, (tq, tq), 1)
              >= lax.broadcasted_iota(jnp.int32, (tq, tq), 0))

    def make_tile(qi, slot, odd_block):
        qs = pl.multiple_of(qi * tq, tq)
        q = [q_ref[pl.ds(qs, tq), lanes(hh)] for hh in range(hps)]
        rq = [LOG2E * ct_ref[hh, qi] for hh in range(hps)]

        def scores(hh, j):
            ks = pl.multiple_of(j * tq, tq)
            return (_dot_nt(k_ref[pl.ds(ks, tq), lanes(hh)], q[hh])
                    + _lane_tile(colb_ref[hh, pl.ds(ks, tq), :], tq))

        def values(hh, j):
            return v_ref[pl.ds(pl.multiple_of(j * tq, tq), tq), lanes(hh)]

        def first(hh):
            tm = _dot_nt(km_ref[:, lanes(hh)], q[hh]) + _lane_tile(colbm_ref[hh], tq)
            ts = [jnp.where(causal, scores(hh, qi), _NEG_INF), jnp.where(meta_valid, tm, _NEG_INF)]
            vs = [values(hh, qi), vm_ref[:, lanes(hh)]]
            if odd_block is not None:
                ts.append(scores(hh, odd_block))
                vs.append(values(hh, odd_block))
            return jnp.concatenate(ts, axis=0), jnp.concatenate(vs, axis=0)

        tile = _QueryTile(hps, d, tq, rq, first, scores, values, t_ref.at[slot])
        tile.qs = qs
        return tile

    def finish(tile, state):
        for hh in range(hps):
            _, l, acc = state[3 * hh:3 * hh + 3]
            o_ref[pl.ds(tile.qs, tq), lanes(hh)] = (acc / l).T.astype(o_ref.dtype)

    _sweep_tile_pairs(nq, make_tile, finish)


def _fox_attention(proj, proj_meta, c, cm, ct, nb, seq, tq=ATT_TILE, hps=FOX_HEADS_PER_STEP):
    m = proj.shape[0]
    w = hps * HEAD_DIM
    qb, kb, vb = (i * FOX_HEADS // hps for i in range(3))
    nq = seq // tq
    tok = lambda off: pl.BlockSpec((seq, w), lambda b, s: (b, off + s))
    meta = lambda off: pl.BlockSpec((META_ROWS, w), lambda b, s: (0, off + s))
    return pl.pallas_call(
        functools.partial(_fox_kernel, tq=tq, hps=hps),
        grid=(nb, FOX_HEADS // hps),
        in_specs=[tok(qb), tok(kb), tok(vb), meta(kb), meta(vb),
                  pl.BlockSpec((seq, LANES), lambda b, s: (b, 0)),
                  pl.BlockSpec((META_ROWS, LANES), lambda b, s: (0, 0)),
                  pl.BlockSpec((hps, nq, 1, tq), lambda b, s: (s, b, 0, 0))],
        out_specs=pl.BlockSpec((seq, w), lambda b, s: (b, s)),
        out_shape=jax.ShapeDtypeStruct((m, FOX_HEADS * HEAD_DIM), _BF16),
        scratch_shapes=[pltpu.VMEM((hps, seq, LANES), _F32),
                        pltpu.VMEM((hps, META_ROWS, LANES), _F32),
                        pltpu.VMEM((2, 2, hps, tq, tq), _F32)],
        compiler_params=_params("parallel", "arbitrary"),
        name="fox_attention",
    )(proj, proj, proj, proj_meta, proj_meta, c, cm, ct)


def _diff_kernel(q_ref, k_ref, v_ref, km_ref, vm_ref, slope_ref, lq1_ref, lk1_ref, lq2_ref,
                 lk2_ref, g_ref, o_ref, colb_ref, t_ref, *, tq, lambda_init):
    nq = q_ref.shape[0] // tq
    d = HEAD_DIM
    comp = lambda c: slice(c * d, (c + 1) * d)
    slope2 = LOG2E * slope_ref[:, 0:1]

    key_row = lax.broadcasted_iota(jnp.int32, (tq, LANES), 0)

    def fill(i, carry):
        rs = pl.multiple_of(i * tq, tq)
        colb_ref[pl.ds(rs, tq), :] = slope2 * (key_row + (rs + N_META)).astype(_F32)
        return carry

    lax.fori_loop(0, nq, fill, 0)
    colbm = slope2 * lax.broadcasted_iota(jnp.int32, (META_ROWS, LANES), 0).astype(_F32)
    query_col = lax.broadcasted_iota(jnp.int32, (1, tq), 1)

    meta_valid = lax.broadcasted_iota(jnp.int32, (META_ROWS, tq), 0) < N_META
    causal = (lax.broadcasted_iota(jnp.int32, (tq, tq), 1)
              >= lax.broadcasted_iota(jnp.int32, (tq, tq), 0))
    lam = (jnp.exp(jnp.sum(lq1_ref[...] * lk1_ref[...], axis=1, keepdims=True))
           - jnp.exp(jnp.sum(lq2_ref[...] * lk2_ref[...], axis=1, keepdims=True)) + lambda_init)

    def make_tile(qi, slot, odd_block):
        qs = pl.multiple_of(qi * tq, tq)
        q = [q_ref[pl.ds(qs, tq), comp(c)] for c in range(2)]
        rq = -slope2 * (query_col + (qs + N_META)).astype(_F32)

        def scores(c, j):
            ks = pl.multiple_of(j * tq, tq)
            return (_dot_nt(k_ref[pl.ds(ks, tq), comp(c)], q[c])
                    + _lane_tile(colb_ref[pl.ds(ks, tq), :], tq))

        def values(c, j):
            return v_ref[pl.ds(pl.multiple_of(j * tq, tq), tq), :]

        def first(c):
            tm = _dot_nt(km_ref[:, comp(c)], q[c]) + _lane_tile(colbm, tq)
            ts = [jnp.where(causal, scores(c, qi), _NEG_INF), jnp.where(meta_valid, tm, _NEG_INF)]
            vs = [values(c, qi), vm_ref[...]]
            if odd_block is not None:
                ts.append(scores(c, odd_block))
                vs.append(values(c, odd_block))
            return jnp.concatenate(ts, axis=0), jnp.concatenate(vs, axis=0)

        tile = _QueryTile(2, 2 * d, tq, (rq, rq), first, scores, values, t_ref.at[slot])
        tile.qs = qs
        return tile

    def finish(tile, state):
        _, l1, a1, _, l2, a2 = state
        o = (a1 / l1 - lam * (a2 / l2)).T
        y = _rms(o, g_ref[...]) * (1.0 - lambda_init)
        o_ref[pl.ds(tile.qs, tq), :] = y.astype(o_ref.dtype)

    _sweep_tile_pairs(nq, make_tile, finish)


def _diff_attention(proj, proj_meta, slopes, lq1, lk1, lq2, lk2, subln_g, lambda_init, nb, seq,
                    tq=ATT_TILE):
    m = proj.shape[0]
    w = 2 * HEAD_DIM
    qb, kb, vb = (i * DIFF_HEADS for i in range(3))
    tok = lambda off: pl.BlockSpec((seq, w), lambda b, h: (b, off + h))
    meta = lambda off: pl.BlockSpec((META_ROWS, w), lambda b, h: (0, off + h))
    vec = lambda n: pl.BlockSpec((1, n), lambda b, h: (0, 0))
    return pl.pallas_call(
        functools.partial(_diff_kernel, tq=tq, lambda_init=lambda_init),
        grid=(nb, DIFF_HEADS),
        in_specs=[tok(qb), tok(kb), tok(vb), meta(kb), meta(vb),
                  pl.BlockSpec((None, 1, LANES), lambda b, h: (h, 0, 0)),
                  vec(HEAD_DIM), vec(HEAD_DIM), vec(HEAD_DIM), vec(HEAD_DIM), vec(w)],
        out_specs=pl.BlockSpec((seq, w), lambda b, h: (b, h)),
        out_shape=jax.ShapeDtypeStruct((m, DIFF_HEADS * w), _BF16),
        scratch_shapes=[pltpu.VMEM((seq, LANES), _F32), pltpu.VMEM((2, 2, 2, tq, tq), _F32)],
        compiler_params=_params("parallel", "arbitrary"),
        name="diff_attention",
    )(proj, proj, proj, proj_meta, proj_meta, slopes, lq1, lk1, lq2, lk2, subln_g)


def _ffn(h, xn, w_gate, w_up, w_down, tm_down):
    hid = _gateup(xn, w_gate, w_up)
    return _mm(hid, w_down, SUBLAYER_OUT_DTYPE, tm=tm_down, tn=512)


def kernel(x, meta_tokens, ff1_pre_g, ff1_w_gate, ff1_w_up, ff1_w_down, ff1_post_g, mix_pre_g, w_in, b_forget, lambda_q1, lambda_k1, lambda_q2, lambda_k2, diff_subln_g, w_o_fox, w_o_diff, w_out, mix_post_g, ff2_pre_g, ff2_w_gate, ff2_w_up, ff2_w_down, ff2_post_g):
    nb, seq, d_model = x.shape
    depth = w_in.shape[0]
    assert depth == 1, "deeper stacks need the meta-token query path"
    h = x.reshape(nb * seq, d_model)
    hm = jnp.pad(meta_tokens.astype(x.dtype), ((0, META_ROWS - N_META), (0, 0)))
    slopes = jnp.exp2(-ALIBI_MAX_BIAS * jnp.arange(1, DIFF_HEADS + 1, dtype=_F32) / DIFF_HEADS)
    slopes = jnp.broadcast_to(slopes[:, None, None], (DIFF_HEADS, 1, LANES))
    nq = seq // ATT_TILE
    fox_w = FOX_HEADS * HEAD_DIM
    diff_w = DIFF_HEADS * 2 * HEAD_DIM
    f_col = 3 * fox_w
    vec = lambda a: a.reshape(1, -1)
    bf = lambda a: a.astype(_BF16)

    for layer in range(depth):
        lambda_init = 0.8 - 0.6 * math.exp(-0.3 * layer)

        wg, wu, wd = ff1_w_gate[layer], ff1_w_up[layer], bf(ff1_w_down[layer])
        pre, post, mix_pre = vec(ff1_pre_g[layer]), vec(ff1_post_g[layer]), vec(mix_pre_g[layer])
        y = _ffn(h, _norm(h, pre), wg, wu, wd, tm_down=512)
        h, u = _resid_norm(h, y, post, mix_pre, 0.5)
        ym = _ffn(hm, _norm(hm, pre), wg, wu, wd, tm_down=META_ROWS)
        hm, um = _resid_norm(hm, ym, post, mix_pre, 0.5)

        w = w_in[layer].T
        d_col = f_col + FOX_HEADS
        proj_f = _proj(u, w, 0, 3 * fox_w, fox_w)
        proj_d = _proj(u, w, d_col, 3 * diff_w, diff_w)
        gates = _proj(u, w, d_col + 3 * diff_w, 2 * d_model, None)
        proj_fm = _proj(um, w, 0, 3 * fox_w, fox_w)
        proj_dm = _proj(um, w, d_col, 3 * diff_w, diff_w)

        b_f = jnp.pad(b_forget[layer].astype(_F32), (0, LANES - FOX_HEADS)).reshape(1, LANES)
        cm, _ = _forget_cumsum(um, w, f_col, b_f, jnp.zeros((1, LANES), _F32), 1, N_META, META_ROWS)
        c, ct = _forget_cumsum(u, w, f_col, b_f, cm[META_ROWS - 1:], nb, CUMSUM_TILE, CUMSUM_TILE)
        ct = ct.reshape(FOX_HEADS, nb * nq, 1, ATT_TILE)

        fox = _fox_attention(proj_f, proj_fm, c, cm, ct, nb, seq)
        diff = _diff_attention(proj_d, proj_dm, slopes, vec(lambda_q1[layer]).astype(_F32),
                               vec(lambda_k1[layer]).astype(_F32), vec(lambda_q2[layer]).astype(_F32),
                               vec(lambda_k2[layer]).astype(_F32), vec(diff_subln_g[layer]).astype(_F32),
                               lambda_init, nb, seq)
        merged = _merge(fox, diff, w_o_fox[layer], w_o_diff[layer], gates, 0, d_model)
        mix = _mm(merged, w_out[layer], SUBLAYER_OUT_DTYPE, tm=1024, tn=512)

        wg, wu, wd = ff2_w_gate[layer], ff2_w_up[layer], bf(ff2_w_down[layer])
        pre2, post2 = vec(ff2_pre_g[layer]), vec(ff2_post_g[layer])
        h, xn = _resid_norm(h, mix, vec(mix_post_g[layer]), pre2, 1.0)
        y = _ffn(h, xn, wg, wu, wd, tm_down=512)
        h = _resid(h, y, post2, 0.5)

    return h.reshape(nb, seq, d_model)
```

```python
import functools
import math

import jax
import jax.numpy as jnp
from jax import lax
from jax.experimental import pallas as pl
from jax.experimental.pallas import tpu as pltpu

N_META = 16
FOX_HEADS = 16
DIFF_HEADS = 8
HEAD_DIM = 128
RMS_EPS = 1e-6
ALIBI_MAX_BIAS = 8.0

LANES = 128
F32_SUBLANES = 8
META_ROWS = 128
VMEM_LIMIT = 58 * 1024 * 1024
ATT_TILE = 512
FOX_HEADS_PER_STEP = 2
LOG2E = math.log2(math.e)
CUMSUM_TILE = 256

_F32 = jnp.float32
_BF16 = jnp.bfloat16
_NEG_INF = float("-inf")
SUBLAYER_OUT_DTYPE = _BF16


def _params(*sem):
    return pltpu.CompilerParams(dimension_semantics=sem, vmem_limit_bytes=VMEM_LIMIT)


def _row_tile(m, want):
    return want if m % want == 0 else m


def _rms(x, g):
    return x * lax.rsqrt(jnp.mean(x * x, axis=-1, keepdims=True) + RMS_EPS) * g


def _sigmoid(x):
    return 0.5 * jnp.tanh(0.5 * x) + 0.5


def _dot(a, b):
    return jnp.dot(a, b, preferred_element_type=_F32)


def _dot_nt(a, b):
    return lax.dot_general(a, b, (((1,), (1,)), ((), ())), preferred_element_type=_F32)


def _norm_kernel(h_ref, g_ref, o_ref):
    o_ref[...] = _rms(h_ref[...], g_ref[...]).astype(o_ref.dtype)


def _norm(h, g, tm=256):
    m, d = h.shape
    tm = _row_tile(m, tm)
    return pl.pallas_call(
        _norm_kernel,
        grid=(m // tm,),
        in_specs=[pl.BlockSpec((tm, d), lambda i: (i, 0)),
                  pl.BlockSpec((1, d), lambda i: (0, 0))],
        out_specs=pl.BlockSpec((tm, d), lambda i: (i, 0)),
        out_shape=jax.ShapeDtypeStruct((m, d), _BF16),
        compiler_params=_params("parallel"),
        name="norm",
    )(h, g)


def _resid_norm_kernel(h_ref, y_ref, gpost_ref, gpre_ref, hn_ref, xn_ref, *, alpha):
    hn = h_ref[...] + alpha * _rms(y_ref[...].astype(_F32), gpost_ref[...])
    hn_ref[...] = hn
    xn_ref[...] = _rms(hn, gpre_ref[...]).astype(xn_ref.dtype)


def _resid_norm(h, y, gpost, gpre, alpha, tm=256):
    m, d = h.shape
    tm = _row_tile(m, tm)
    row = pl.BlockSpec((tm, d), lambda i: (i, 0))
    vec = pl.BlockSpec((1, d), lambda i: (0, 0))
    return pl.pallas_call(
        functools.partial(_resid_norm_kernel, alpha=alpha),
        grid=(m // tm,),
        in_specs=[row, row, vec, vec],
        out_specs=[row, row],
        out_shape=[jax.ShapeDtypeStruct((m, d), _F32), jax.ShapeDtypeStruct((m, d), _BF16)],
        compiler_params=_params("parallel"),
        name="resid_norm",
    )(h, y, gpost, gpre)


def _resid_kernel(h_ref, y_ref, gpost_ref, o_ref, *, alpha):
    o_ref[...] = h_ref[...] + alpha * _rms(y_ref[...].astype(_F32), gpost_ref[...])


def _resid(h, y, gpost, alpha, tm=256):
    m, d = h.shape
    tm = _row_tile(m, tm)
    row = pl.BlockSpec((tm, d), lambda i: (i, 0))
    vec = pl.BlockSpec((1, d), lambda i: (0, 0))
    return pl.pallas_call(
        functools.partial(_resid_kernel, alpha=alpha),
        grid=(m // tm,),
        in_specs=[row, row, vec],
        out_specs=row,
        out_shape=jax.ShapeDtypeStruct((m, d), _F32),
        compiler_params=_params("parallel"),
        name="resid",
    )(h, y, gpost)


def _mm_kernel(a_ref, w_ref, o_ref):
    o_ref[...] = _dot(a_ref[...], w_ref[...].astype(a_ref.dtype)).astype(o_ref.dtype)


def _mm(a, w, out_dtype, tm, tn):
    m, k = a.shape
    n = w.shape[1]
    tm = _row_tile(m, tm)
    return pl.pallas_call(
        _mm_kernel,
        grid=(m // tm, n // tn),
        in_specs=[pl.BlockSpec((tm, k), lambda i, j: (i, 0)),
                  pl.BlockSpec((k, tn), lambda i, j: (0, j))],
        out_specs=pl.BlockSpec((tm, tn), lambda i, j: (i, j)),
        out_shape=jax.ShapeDtypeStruct((m, n), out_dtype),
        compiler_params=_params("parallel", "arbitrary"),
        name="mm",
    )(a, w)


def _gateup_kernel(x_ref, wg_ref, wu_ref, o_ref):
    x = x_ref[...]
    g = _dot(x, wg_ref[...].astype(x.dtype))
    u = _dot(x, wu_ref[...].astype(x.dtype))
    o_ref[...] = (g * _sigmoid(g) * u).astype(o_ref.dtype)


def _gateup(x, wg, wu, tm=2048, tn=256):
    m, k = x.shape
    n = wg.shape[1]
    tm = _row_tile(m, tm)
    wspec = pl.BlockSpec((k, tn), lambda i, j: (0, j))
    return pl.pallas_call(
        _gateup_kernel,
        grid=(m // tm, n // tn),
        in_specs=[pl.BlockSpec((tm, k), lambda i, j: (i, 0), pipeline_mode=pl.Buffered(1)),
                  wspec, wspec],
        out_specs=pl.BlockSpec((tm, tn), lambda i, j: (i, j)),
        out_shape=jax.ShapeDtypeStruct((m, n), _BF16),
        compiler_params=_params("parallel", "arbitrary"),
        name="gateup",
    )(x, wg, wu)


def _proj_kernel(u_ref, wt_ref, o_ref, *, tn, q_cols, q_scale):
    acc = _dot_nt(u_ref[...], wt_ref[...].astype(u_ref.dtype))
    if q_cols is None:
        o_ref[...] = _sigmoid(acc).astype(o_ref.dtype)
    else:
        scale = jnp.where(pl.program_id(1) * tn < q_cols, q_scale, 1.0).astype(_F32)
        o_ref[...] = (acc * scale).astype(o_ref.dtype)


def _proj(u, w_in_t, col0, n_out, q_cols, tm=1024, tn=512):
    m, k = u.shape
    tm = _row_tile(m, tm)
    return pl.pallas_call(
        functools.partial(_proj_kernel, tn=tn, q_cols=q_cols, q_scale=HEAD_DIM ** -0.5 * LOG2E),
        grid=(m // tm, n_out // tn),
        in_specs=[pl.BlockSpec((tm, k), lambda i, j: (i, 0)),
                  pl.BlockSpec((pl.Element(tn), pl.Element(k)),
                               lambda i, j: (pl.multiple_of(col0 + j * tn, F32_SUBLANES), 0))],
        out_specs=pl.BlockSpec((tm, tn), lambda i, j: (i, j)),
        out_shape=jax.ShapeDtypeStruct((m, n_out), _BF16),
        compiler_params=_params("parallel", "arbitrary"),
        name="proj_gate" if q_cols is None else "proj_qkv",
    )(u, w_in_t)


def _merge_kernel(af_ref, ad_ref, wf_ref, wd_ref, gf_ref, gd_ref, o_ref):
    yf = _dot(af_ref[...], wf_ref[...].astype(af_ref.dtype))
    yd = _dot(ad_ref[...], wd_ref[...].astype(ad_ref.dtype))
    o_ref[...] = (gf_ref[...].astype(_F32) * yf + gd_ref[...].astype(_F32) * yd).astype(o_ref.dtype)


def _merge(fox, diff, w_of, w_od, gates, gf_col, gd_col, tm=1024, tn=512):
    m, k = fox.shape
    n = w_of.shape[1]
    aspec = pl.BlockSpec((tm, k), lambda i, j: (i, 0))
    wspec = pl.BlockSpec((k, tn), lambda i, j: (0, j))
    return pl.pallas_call(
        _merge_kernel,
        grid=(m // tm, n // tn),
        in_specs=[aspec, aspec, wspec, wspec,
                  pl.BlockSpec((tm, tn), lambda i, j: (i, gf_col // tn + j)),
                  pl.BlockSpec((tm, tn), lambda i, j: (i, gd_col // tn + j))],
        out_specs=pl.BlockSpec((tm, tn), lambda i, j: (i, j)),
        out_shape=jax.ShapeDtypeStruct((m, n), _BF16),
        compiler_params=_params("parallel", "arbitrary"),
        name="merge",
    )(fox, diff, w_of, w_od, gates, gates)


def _cumsum_kernel(u_ref, wf_ref, bf_ref, init_ref, c_ref, ct_ref, carry_ref, *, n_valid):
    tm = u_ref.shape[0]

    @pl.when(pl.program_id(1) == 0)
    def _():
        carry_ref[...] = init_ref[...]

    is_head = lax.broadcasted_iota(jnp.int32, wf_ref.shape, 0) < FOX_HEADS
    wf = jnp.where(is_head, wf_ref[...], 0.0).astype(u_ref.dtype)
    x = _dot_nt(u_ref[...], wf) + bf_ref[...]
    logf = jnp.minimum(x, 0.0) - jnp.log1p(jnp.exp(-jnp.abs(x)))
    row = lax.broadcasted_iota(jnp.int32, (tm, tm), 0)
    col = lax.broadcasted_iota(jnp.int32, (tm, tm), 1)
    if n_valid < tm:
        logf = jnp.where(lax.broadcasted_iota(jnp.int32, (tm, LANES), 0) < n_valid, logf, 0.0)
    tri = (row >= col).astype(_F32)
    c = jnp.dot(tri, logf, precision=lax.Precision.HIGHEST, preferred_element_type=_F32)
    c = c + carry_ref[...]
    c_ref[...] = c
    ct_ref[...] = c.T[:FOX_HEADS, :]
    carry_ref[...] = c[tm - 1:tm, :]


def _forget_cumsum(u, w_in_t, f_col, bf, init, nb, n_valid, tm):
    m, k = u.shape
    nt = m // (nb * tm)
    f_blk = f_col // LANES
    assert f_blk * LANES == f_col
    return pl.pallas_call(
        functools.partial(_cumsum_kernel, n_valid=n_valid),
        grid=(nb, nt),
        in_specs=[pl.BlockSpec((tm, k), lambda b, t: (b * nt + t, 0)),
                  pl.BlockSpec((LANES, k), lambda b, t: (f_blk, 0)),
                  pl.BlockSpec((1, LANES), lambda b, t: (0, 0)),
                  pl.BlockSpec((1, LANES), lambda b, t: (0, 0))],
        out_specs=[pl.BlockSpec((tm, LANES), lambda b, t: (b * nt + t, 0)),
                   pl.BlockSpec((FOX_HEADS, tm), lambda b, t: (0, b * nt + t))],
        out_shape=[jax.ShapeDtypeStruct((m, LANES), _F32),
                   jax.ShapeDtypeStruct((FOX_HEADS, m), _F32)],
        scratch_shapes=[pltpu.VMEM((1, LANES), _F32)],
        compiler_params=_params("arbitrary", "arbitrary"),
        name="forget_cumsum",
    )(u, w_in_t, bf, init)


def _dot_tn(a, b):
    return lax.dot_general(a, b, (((0,), (0,)), ((), ())), preferred_element_type=_F32)


def _lane_tile(x, n):
    return jnp.concatenate([x] * (n // LANES), axis=1)


def _flash_step(t, v, rq, m, l, acc):
    m_new = jnp.maximum(m, jnp.max(t, axis=0, keepdims=True) + rq)
    alpha = jnp.exp2(m - m_new)
    p = jnp.exp2(t - (m_new - rq))
    l = alpha * l + jnp.sum(p, axis=0, keepdims=True)
    acc = alpha * acc + _dot_tn(v, p.astype(v.dtype))
    return m_new, l, acc


def _flash_init(dv, tq):
    return (jnp.full((1, tq), _NEG_INF, _F32), jnp.zeros((1, tq), _F32), jnp.zeros((dv, tq), _F32))


def _block_start(j, size):
    return j * size if isinstance(j, int) else pl.multiple_of(j * size, size)


class _QueryTile:
    def __init__(self, streams, dv, tq, rq, first_scores, first_values, scores, values,
                 first_ref, t_ref):
        self.streams, self.dv, self.tq, self.rq = streams, dv, tq, rq
        self.first_scores, self.first_values = first_scores, first_values
        self.scores, self.values, self.first_ref, self.t_ref = scores, values, first_ref, t_ref

    def stage_first(self):
        for s in range(self.streams):
            t = self.first_scores(s)
            self.first_ref[s, :t.shape[0], :] = t
        self.first_keys = t.shape[0]

    def start(self, n_pairs):
        if n_pairs:
            for s in range(self.streams):
                self.t_ref[0, s] = self.scores(s, 0)
        state = []
        for s in range(self.streams):
            state.extend(_flash_step(self.first_ref[s, :self.first_keys, :], self.first_values(s),
                                     self.rq[s], *_flash_init(self.dv, self.tq)))
        return tuple(state)

    def run_pairs(self, n_pairs, state):
        def half(st, slot, blk, nxt):
            for s in range(self.streams):
                self.t_ref[1 - slot, s] = self.scores(s, nxt)
            out = []
            for s in range(self.streams):
                out.extend(_flash_step(self.t_ref[slot, s], self.values(s, blk), self.rq[s],
                                       *st[3 * s:3 * s + 3]))
            return tuple(out)

        def pair(j, st):
            st = half(st, 0, 2 * j, 2 * j + 1)
            return half(st, 1, 2 * j + 1, jnp.minimum(2 * j + 2, 2 * n_pairs - 1))

        return lax.fori_loop(0, n_pairs, pair, state) if n_pairs else state


def _sweep_tile_pairs(n_tiles, make_tile, finish):
    def pair_tiles(a):
        tiles = [make_tile(2 * a, 0, None, a % 2), make_tile(2 * a + 1, 1, 2 * a, a % 2)]
        for t in tiles:
            t.stage_first()
        return tiles

    tiles = pair_tiles(0)
    for a in range(n_tiles // 2):
        nxt = pair_tiles(a + 1) if a + 1 < n_tiles // 2 else None
        states = [t.start(a) for t in tiles]
        states = [t.run_pairs(a, st) for t, st in zip(tiles, states)]
        for t, st in zip(tiles, states):
            finish(t, st)
        tiles = nxt


def _fox_kernel(q_ref, k_ref, v_ref, km_ref, vm_ref, c_ref, cm_ref, ct_ref, o_ref, colb_ref,
                colbm_ref, t_ref, first_even_ref, first_odd_ref, *, tq, hps):
    nq = q_ref.shape[0] // tq
    d = HEAD_DIM
    heads = [pl.program_id(1) * hps + hh for hh in range(hps)]
    lanes = lambda hh: slice(hh * d, (hh + 1) * d)

    lane = lax.broadcasted_iota(jnp.int32, (tq, LANES), 1)
    lane_m = lax.broadcasted_iota(jnp.int32, (META_ROWS, LANES), 1)
    for hh in range(hps):
        def fill(i, carry, hh=hh):
            rs = pl.multiple_of(i * tq, tq)
            ck = jnp.sum(jnp.where(lane == heads[hh], c_ref[pl.ds(rs, tq), :], 0.0),
                         axis=1, keepdims=True)
            colb_ref[hh, pl.ds(rs, tq), :] = jnp.broadcast_to(-LOG2E * ck, (tq, LANES))
            return carry

        lax.fori_loop(0, nq, fill, 0)
        ckm = jnp.sum(jnp.where(lane_m == heads[hh], cm_ref[...], 0.0), axis=1, keepdims=True)
        colbm_ref[hh] = jnp.broadcast_to(-LOG2E * ckm, (META_ROWS, LANES))

    meta_valid = lax.broadcasted_iota(jnp.int32, (META_ROWS, tq), 0) < N_META
    causal = (lax.broadcasted_iota(jnp.int32, (tq, tq), 1)
              >= lax.broadcasted_iota(jnp.int32, (tq, tq), 0))

    def make_tile(qi, slot, odd_block, stage_slot):
        qs = qi * tq
        q = [q_ref[pl.ds(qs, tq), lanes(hh)] for hh in range(hps)]
        rq = [LOG2E * ct_ref[hh, qi] for hh in range(hps)]

        def scores(hh, j):
            ks = _block_start(j, tq)
            return (_dot_nt(k_ref[pl.ds(ks, tq), lanes(hh)], q[hh])
                    + _lane_tile(colb_ref[hh, pl.ds(ks, tq), :], tq))

        def values(hh, j):
            return v_ref[pl.ds(_block_start(j, tq), tq), lanes(hh)]

        def first_scores(hh):
            tm = _dot_nt(km_ref[:, lanes(hh)], q[hh]) + _lane_tile(colbm_ref[hh], tq)
            ts = [jnp.where(causal, scores(hh, qi), _NEG_INF), jnp.where(meta_valid, tm, _NEG_INF)]
            if odd_block is not None:
                ts.append(scores(hh, odd_block))
            return jnp.concatenate(ts, axis=0)

        def first_values(hh):
            vs = [values(hh, qi), vm_ref[:, lanes(hh)]]
            if odd_block is not None:
                vs.append(values(hh, odd_block))
            return jnp.concatenate(vs, axis=0)

        first_ref = (first_even_ref if odd_block is None else first_odd_ref).at[stage_slot]
        tile = _QueryTile(hps, d, tq, rq, first_scores, first_values, scores, values, first_ref,
                          t_ref.at[slot])
        tile.qs = qs
        return tile

    def finish(tile, state):
        for hh in range(hps):
            _, l, acc = state[3 * hh:3 * hh + 3]
            o_ref[pl.ds(tile.qs, tq), lanes(hh)] = (acc / l).T.astype(o_ref.dtype)

    _sweep_tile_pairs(nq, make_tile, finish)


def _fox_attention(proj, proj_meta, c, cm, ct, nb, seq, tq=ATT_TILE, hps=FOX_HEADS_PER_STEP):
    m = proj.shape[0]
    w = hps * HEAD_DIM
    qb, kb, vb = (i * FOX_HEADS // hps for i in range(3))
    nq = seq // tq
    tok = lambda off: pl.BlockSpec((seq, w), lambda b, s: (b, off + s))
    meta = lambda off: pl.BlockSpec((META_ROWS, w), lambda b, s: (0, off + s))
    return pl.pallas_call(
        functools.partial(_fox_kernel, tq=tq, hps=hps),
        grid=(nb, FOX_HEADS // hps),
        in_specs=[tok(qb), tok(kb), tok(vb), meta(kb), meta(vb),
                  pl.BlockSpec((seq, LANES), lambda b, s: (b, 0), pipeline_mode=pl.Buffered(1)),
                  pl.BlockSpec((META_ROWS, LANES), lambda b, s: (0, 0)),
                  pl.BlockSpec((hps, nq, 1, tq), lambda b, s: (s, b, 0, 0))],
        out_specs=pl.BlockSpec((seq, w), lambda b, s: (b, s)),
        out_shape=jax.ShapeDtypeStruct((m, FOX_HEADS * HEAD_DIM), _BF16),
        scratch_shapes=[pltpu.VMEM((hps, seq, LANES), _F32),
                        pltpu.VMEM((hps, META_ROWS, LANES), _F32),
                        pltpu.VMEM((2, 2, hps, tq, tq), _F32),
                        pltpu.VMEM((2, hps, tq + META_ROWS, tq), _F32),
                        pltpu.VMEM((2, hps, 2 * tq + META_ROWS, tq), _F32)],
        compiler_params=_params("parallel", "arbitrary"),
        name="fox_attention",
    )(proj, proj, proj, proj_meta, proj_meta, c, cm, ct)


def _diff_kernel(q_ref, k_ref, v_ref, km_ref, vm_ref, slope_ref, lq1_ref, lk1_ref, lq2_ref,
                 lk2_ref, g_ref, o_ref, colb_ref, t_ref, first_even_ref, first_odd_ref, *, tq,
                 lambda_init):
    nq = q_ref.shape[0] // tq
    d = HEAD_DIM
    comp = lambda c: slice(c * d, (c + 1) * d)
    slope2 = LOG2E * slope_ref[:, 0:1]

    key_row = lax.broadcasted_iota(jnp.int32, (tq, LANES), 0)

    def fill(i, carry):
        rs = pl.multiple_of(i * tq, tq)
        colb_ref[pl.ds(rs, tq), :] = slope2 * (key_row + (rs + N_META)).astype(_F32)
        return carry

    lax.fori_loop(0, nq, fill, 0)
    colbm = slope2 * lax.broadcasted_iota(jnp.int32, (META_ROWS, LANES), 0).astype(_F32)
    query_col = lax.broadcasted_iota(jnp.int32, (1, tq), 1)

    meta_valid = lax.broadcasted_iota(jnp.int32, (META_ROWS, tq), 0) < N_META
    causal = (lax.broadcasted_iota(jnp.int32, (tq, tq), 1)
              >= lax.broadcasted_iota(jnp.int32, (tq, tq), 0))
    lam = (jnp.exp(jnp.sum(lq1_ref[...] * lk1_ref[...], axis=1, keepdims=True))
           - jnp.exp(jnp.sum(lq2_ref[...] * lk2_ref[...], axis=1, keepdims=True)) + lambda_init)

    def make_tile(qi, slot, odd_block, stage_slot):
        qs = qi * tq
        q = [q_ref[pl.ds(qs, tq), comp(c)] for c in range(2)]
        rq = -slope2 * (query_col + (qs + N_META)).astype(_F32)

        def scores(c, j):
            ks = _block_start(j, tq)
            return (_dot_nt(k_ref[pl.ds(ks, tq), comp(c)], q[c])
                    + _lane_tile(colb_ref[pl.ds(ks, tq), :], tq))

        def values(c, j):
            return v_ref[pl.ds(_block_start(j, tq), tq), :]

        def first_scores(c):
            tm = _dot_nt(km_ref[:, comp(c)], q[c]) + _lane_tile(colbm, tq)
            ts = [jnp.where(causal, scores(c, qi), _NEG_INF), jnp.where(meta_valid, tm, _NEG_INF)]
            if odd_block is not None:
                ts.append(scores(c, odd_block))
            return jnp.concatenate(ts, axis=0)

        def first_values(c):
            vs = [values(c, qi), vm_ref[...]]
            if odd_block is not None:
                vs.append(values(c, odd_block))
            return jnp.concatenate(vs, axis=0)

        first_ref = (first_even_ref if odd_block is None else first_odd_ref).at[stage_slot]
        tile = _QueryTile(2, 2 * d, tq, (rq, rq), first_scores, first_values, scores, values,
                          first_ref, t_ref.at[slot])
        tile.qs = qs
        return tile

    def finish(tile, state):
        _, l1, a1, _, l2, a2 = state
        o = (a1 / l1 - lam * (a2 / l2)).T
        y = _rms(o, g_ref[...]) * (1.0 - lambda_init)
        o_ref[pl.ds(tile.qs, tq), :] = y.astype(o_ref.dtype)

    _sweep_tile_pairs(nq, make_tile, finish)


def _diff_attention(proj, proj_meta, slopes, lq1, lk1, lq2, lk2, subln_g, lambda_init, nb, seq,
                    tq=ATT_TILE):
    m = proj.shape[0]
    w = 2 * HEAD_DIM
    qb, kb, vb = (i * DIFF_HEADS for i in range(3))
    tok = lambda off: pl.BlockSpec((seq, w), lambda b, h: (b, off + h))
    meta = lambda off: pl.BlockSpec((META_ROWS, w), lambda b, h: (0, off + h))
    vec = lambda n: pl.BlockSpec((1, n), lambda b, h: (0, 0))
    return pl.pallas_call(
        functools.partial(_diff_kernel, tq=tq, lambda_init=lambda_init),
        grid=(nb, DIFF_HEADS),
        in_specs=[tok(qb), tok(kb), tok(vb), meta(kb), meta(vb),
                  pl.BlockSpec((None, 1, LANES), lambda b, h: (h, 0, 0)),
                  vec(HEAD_DIM), vec(HEAD_DIM), vec(HEAD_DIM), vec(HEAD_DIM), vec(w)],
        out_specs=pl.BlockSpec((seq, w), lambda b, h: (b, h)),
        out_shape=jax.ShapeDtypeStruct((m, DIFF_HEADS * w), _BF16),
        scratch_shapes=[pltpu.VMEM((seq, LANES), _F32), pltpu.VMEM((2, 2, 2, tq, tq), _F32),
                        pltpu.VMEM((2, 2, tq + META_ROWS, tq), _F32),
                        pltpu.VMEM((2, 2, 2 * tq + META_ROWS, tq), _F32)],
        compiler_params=_params("parallel", "arbitrary"),
        name="diff_attention",
    )(proj, proj, proj, proj_meta, proj_meta, slopes, lq1, lk1, lq2, lk2, subln_g)


def _ffn(h, xn, w_gate, w_up, w_down, tm_down):
    hid = _gateup(xn, w_gate, w_up)
    return _mm(hid, w_down, SUBLAYER_OUT_DTYPE, tm=tm_down, tn=512)


def kernel(x, meta_tokens, ff1_pre_g, ff1_w_gate, ff1_w_up, ff1_w_down, ff1_post_g, mix_pre_g, w_in, b_forget, lambda_q1, lambda_k1, lambda_q2, lambda_k2, diff_subln_g, w_o_fox, w_o_diff, w_out, mix_post_g, ff2_pre_g, ff2_w_gate, ff2_w_up, ff2_w_down, ff2_post_g):
    nb, seq, d_model = x.shape
    depth = w_in.shape[0]
    assert depth == 1, "deeper stacks need the meta-token query path"
    h = x.reshape(nb * seq, d_model)
    hm = jnp.pad(meta_tokens.astype(x.dtype), ((0, META_ROWS - N_META), (0, 0)))
    slopes = jnp.exp2(-ALIBI_MAX_BIAS * jnp.arange(1, DIFF_HEADS + 1, dtype=_F32) / DIFF_HEADS)
    slopes = jnp.broadcast_to(slopes[:, None, None], (DIFF_HEADS, 1, LANES))
    nq = seq // ATT_TILE
    fox_w = FOX_HEADS * HEAD_DIM
    diff_w = DIFF_HEADS * 2 * HEAD_DIM
    f_col = 3 * fox_w
    vec = lambda a: a.reshape(1, -1)
    bf = lambda a: a.astype(_BF16)

    for layer in range(depth):
        lambda_init = 0.8 - 0.6 * math.exp(-0.3 * layer)

        wg, wu, wd = ff1_w_gate[layer], ff1_w_up[layer], bf(ff1_w_down[layer])
        pre, post, mix_pre = vec(ff1_pre_g[layer]), vec(ff1_post_g[layer]), vec(mix_pre_g[layer])
        y = _ffn(h, _norm(h, pre), wg, wu, wd, tm_down=512)
        h, u = _resid_norm(h, y, post, mix_pre, 0.5)
        ym = _ffn(hm, _norm(hm, pre), wg, wu, wd, tm_down=META_ROWS)
        hm, um = _resid_norm(hm, ym, post, mix_pre, 0.5)

        w = w_in[layer].T
        d_col = f_col + FOX_HEADS
        proj_f = _proj(u, w, 0, 3 * fox_w, fox_w)
        proj_d = _proj(u, w, d_col, 3 * diff_w, diff_w)
        gates = _proj(u, w, d_col + 3 * diff_w, 2 * d_model, None)
        proj_fm = _proj(um, w, 0, 3 * fox_w, fox_w)
        proj_dm = _proj(um, w, d_col, 3 * diff_w, diff_w)

        b_f = jnp.pad(b_forget[layer].astype(_F32), (0, LANES - FOX_HEADS)).reshape(1, LANES)
        cm, _ = _forget_cumsum(um, w, f_col, b_f, jnp.zeros((1, LANES), _F32), 1, N_META, META_ROWS)
        c, ct = _forget_cumsum(u, w, f_col, b_f, cm[META_ROWS - 1:], nb, CUMSUM_TILE, CUMSUM_TILE)
        ct = ct.reshape(FOX_HEADS, nb * nq, 1, ATT_TILE)

        fox = _fox_attention(proj_f, proj_fm, c, cm, ct, nb, seq)
        diff = _diff_attention(proj_d, proj_dm, slopes, vec(lambda_q1[layer]).astype(_F32),
                               vec(lambda_k1[layer]).astype(_F32), vec(lambda_q2[layer]).astype(_F32),
                               vec(lambda_k2[layer]).astype(_F32), vec(diff_subln_g[layer]).astype(_F32),
                               lambda_init, nb, seq)
        merged = _merge(fox, diff, w_o_fox[layer], w_o_diff[layer], gates, 0, d_model)
        mix = _mm(merged, w_out[layer], SUBLAYER_OUT_DTYPE, tm=1024, tn=512)

        wg, wu, wd = ff2_w_gate[layer], ff2_w_up[layer], bf(ff2_w_down[layer])
        pre2, post2 = vec(ff2_pre_g[layer]), vec(ff2_post_g[layer])
        h, xn = _resid_norm(h, mix, vec(mix_post_g[layer]), pre2, 1.0)
        y = _ffn(h, xn, wg, wu, wd, tm_down=512)
        h = _resid(h, y, post2, 0.5)

    return h.reshape(nb, seq, d_model)
```

```python
import functools
import math

import jax
import jax.numpy as jnp
from jax import lax
from jax.experimental import pallas as pl
from jax.experimental.pallas import tpu as pltpu

N_META = 16
FOX_HEADS = 16
DIFF_HEADS = 8
HEAD_DIM = 128
RMS_EPS = 1e-6
ALIBI_MAX_BIAS = 8.0

LANES = 128
F32_SUBLANES = 8
META_ROWS = 128
VMEM_LIMIT = 58 * 1024 * 1024
ATT_TILE = 512
FOX_HEADS_PER_STEP = 2
LOG2E = math.log2(math.e)
CUMSUM_TILE = 256

_F32 = jnp.float32
_BF16 = jnp.bfloat16
_NEG_INF = float("-inf")
SUBLAYER_OUT_DTYPE = _BF16


def _params(*sem):
    return pltpu.CompilerParams(dimension_semantics=sem, vmem_limit_bytes=VMEM_LIMIT)


def _row_tile(m, want):
    return want if m % want == 0 else m


def _rms(x, g):
    return x * lax.rsqrt(jnp.mean(x * x, axis=-1, keepdims=True) + RMS_EPS) * g


def _sigmoid(x):
    return 0.5 * jnp.tanh(0.5 * x) + 0.5


def _dot(a, b):
    return jnp.dot(a, b, preferred_element_type=_F32)


def _dot_nt(a, b):
    return lax.dot_general(a, b, (((1,), (1,)), ((), ())), preferred_element_type=_F32)


def _norm_kernel(h_ref, g_ref, o_ref):
    o_ref[...] = _rms(h_ref[...], g_ref[...]).astype(o_ref.dtype)


def _norm(h, g, tm=256):
    m, d = h.shape
    tm = _row_tile(m, tm)
    return pl.pallas_call(
        _norm_kernel,
        grid=(m // tm,),
        in_specs=[pl.BlockSpec((tm, d), lambda i: (i, 0)),
                  pl.BlockSpec((1, d), lambda i: (0, 0))],
        out_specs=pl.BlockSpec((tm, d), lambda i: (i, 0)),
        out_shape=jax.ShapeDtypeStruct((m, d), _BF16),
        compiler_params=_params("parallel"),
        name="norm",
    )(h, g)


def _resid_norm_kernel(h_ref, y_ref, gpost_ref, gpre_ref, hn_ref, xn_ref, *, alpha):
    hn = h_ref[...] + alpha * _rms(y_ref[...].astype(_F32), gpost_ref[...])
    hn_ref[...] = hn
    xn_ref[...] = _rms(hn, gpre_ref[...]).astype(xn_ref.dtype)


def _resid_norm(h, y, gpost, gpre, alpha, tm=256):
    m, d = h.shape
    tm = _row_tile(m, tm)
    row = pl.BlockSpec((tm, d), lambda i: (i, 0))
    vec = pl.BlockSpec((1, d), lambda i: (0, 0))
    return pl.pallas_call(
        functools.partial(_resid_norm_kernel, alpha=alpha),
        grid=(m // tm,),
        in_specs=[row, row, vec, vec],
        out_specs=[row, row],
        out_shape=[jax.ShapeDtypeStruct((m, d), _F32), jax.ShapeDtypeStruct((m, d), _BF16)],
        compiler_params=_params("parallel"),
        name="resid_norm",
    )(h, y, gpost, gpre)


def _resid_kernel(h_ref, y_ref, gpost_ref, o_ref, *, alpha):
    o_ref[...] = h_ref[...] + alpha * _rms(y_ref[...].astype(_F32), gpost_ref[...])


def _resid(h, y, gpost, alpha, tm=256):
    m, d = h.shape
    tm = _row_tile(m, tm)
    row = pl.BlockSpec((tm, d), lambda i: (i, 0))
    vec = pl.BlockSpec((1, d), lambda i: (0, 0))
    return pl.pallas_call(
        functools.partial(_resid_kernel, alpha=alpha),
        grid=(m // tm,),
        in_specs=[row, row, vec],
        out_specs=row,
        out_shape=jax.ShapeDtypeStruct((m, d), _F32),
        compiler_params=_params("parallel"),
        name="resid",
    )(h, y, gpost)


def _with_meta_rider(body, n_weights, x, x_meta, weights, row_spec, wspecs, out_dtype, tm, tn, n,
                     name):
    m, k = x.shape
    n_cols = n // tn

    def kernel(x_ref, *refs):
        w = [r[...] for r in refs[:n_weights]]
        if x_meta is None:
            (o_ref,) = refs[n_weights:]
        else:
            xm_ref, o_ref, om_ref = refs[n_weights:]

            def meta():
                om_ref[...] = body(xm_ref[...], *w).astype(om_ref.dtype)

            pl.when(pl.program_id(0) == 0)(meta)
        o_ref[...] = body(x_ref[...], *w).astype(o_ref.dtype)

    in_specs = [row_spec, *wspecs]
    out_specs = [pl.BlockSpec((tm, tn), lambda i, j: (i, j))]
    out_shape = [jax.ShapeDtypeStruct((m, n), out_dtype)]
    args = [x, *weights]
    if x_meta is not None:
        in_specs.append(pl.BlockSpec((META_ROWS, k), lambda i, j: (0, 0)))
        out_specs.append(pl.BlockSpec((META_ROWS, tn),
                                      lambda i, j: (0, jnp.where(i == 0, j, n_cols - 1))))
        out_shape.append(jax.ShapeDtypeStruct((META_ROWS, n), out_dtype))
        args.append(x_meta)
    out = pl.pallas_call(
        kernel,
        grid=(m // tm, n_cols),
        in_specs=in_specs,
        out_specs=out_specs,
        out_shape=out_shape,
        compiler_params=_params("arbitrary", "arbitrary"),
        name=name,
    )(*args)
    return (out[0], out[1]) if x_meta is not None else (out[0], None)


def _mm(a, a_meta, w, out_dtype, tm, tn):
    k, n = w.shape
    return _with_meta_rider(
        lambda x, wb: _dot(x, wb.astype(x.dtype)), 1, a, a_meta, [w],
        pl.BlockSpec((tm, k), lambda i, j: (i, 0)), [pl.BlockSpec((k, tn), lambda i, j: (0, j))],
        out_dtype, tm, tn, n, "mm")


def _silu_gate(x, wg, wu):
    g = _dot(x, wg.astype(x.dtype))
    return g * _sigmoid(g) * _dot(x, wu.astype(x.dtype))


def _gateup(x, x_meta, wg, wu, tm=2048, tn=256):
    k, n = wg.shape
    wspec = pl.BlockSpec((k, tn), lambda i, j: (0, j))
    return _with_meta_rider(
        _silu_gate, 2, x, x_meta, [wg, wu],
        pl.BlockSpec((tm, k), lambda i, j: (i, 0), pipeline_mode=pl.Buffered(1)), [wspec, wspec],
        _BF16, tm, tn, n, "gateup")


def _proj(u, u_meta, w_in_t, col0, n_out, q_cols, tm=1024, tn=512):
    k = u.shape[1]

    def body(x, wt):
        acc = _dot_nt(x, wt.astype(x.dtype))
        if q_cols is None:
            return _sigmoid(acc)
        q_scale = HEAD_DIM ** -0.5 * LOG2E
        return acc * jnp.where(pl.program_id(1) * tn < q_cols, q_scale, 1.0).astype(_F32)

    wspec = pl.BlockSpec((pl.Element(tn), pl.Element(k)),
                         lambda i, j: (pl.multiple_of(col0 + j * tn, F32_SUBLANES), 0))
    return _with_meta_rider(body, 1, u, u_meta, [w_in_t], pl.BlockSpec((tm, k), lambda i, j: (i, 0)),
                            [wspec], _BF16, tm, tn, n_out,
                            "proj_gate" if q_cols is None else "proj_qkv")


def _merge_kernel(af_ref, ad_ref, wf_ref, wd_ref, gf_ref, gd_ref, o_ref):
    yf = _dot(af_ref[...], wf_ref[...].astype(af_ref.dtype))
    yd = _dot(ad_ref[...], wd_ref[...].astype(ad_ref.dtype))
    o_ref[...] = (gf_ref[...].astype(_F32) * yf + gd_ref[...].astype(_F32) * yd).astype(o_ref.dtype)


def _merge(fox, diff, w_of, w_od, gates, gf_col, gd_col, tm=1024, tn=512):
    m, k = fox.shape
    n = w_of.shape[1]
    aspec = pl.BlockSpec((tm, k), lambda i, j: (i, 0))
    wspec = pl.BlockSpec((k, tn), lambda i, j: (0, j))
    return pl.pallas_call(
        _merge_kernel,
        grid=(m // tm, n // tn),
        in_specs=[aspec, aspec, wspec, wspec,
                  pl.BlockSpec((tm, tn), lambda i, j: (i, gf_col // tn + j)),
                  pl.BlockSpec((tm, tn), lambda i, j: (i, gd_col // tn + j))],
        out_specs=pl.BlockSpec((tm, tn), lambda i, j: (i, j)),
        out_shape=jax.ShapeDtypeStruct((m, n), _BF16),
        compiler_params=_params("parallel", "arbitrary"),
        name="merge",
    )(fox, diff, w_of, w_od, gates, gates)


def _cumsum_kernel(u_ref, wf_ref, bf_ref, init_ref, c_ref, ct_ref, carry_ref, *, n_valid):
    tm = u_ref.shape[0]

    @pl.when(pl.program_id(1) == 0)
    def _():
        carry_ref[...] = init_ref[...]

    is_head = lax.broadcasted_iota(jnp.int32, wf_ref.shape, 0) < FOX_HEADS
    wf = jnp.where(is_head, wf_ref[...], 0.0).astype(u_ref.dtype)
    x = _dot_nt(u_ref[...], wf) + bf_ref[...]
    logf = jnp.minimum(x, 0.0) - jnp.log1p(jnp.exp(-jnp.abs(x)))
    row = lax.broadcasted_iota(jnp.int32, (tm, tm), 0)
    col = lax.broadcasted_iota(jnp.int32, (tm, tm), 1)
    if n_valid < tm:
        logf = jnp.where(lax.broadcasted_iota(jnp.int32, (tm, LANES), 0) < n_valid, logf, 0.0)
    tri = (row >= col).astype(_F32)
    c = jnp.dot(tri, logf, precision=lax.Precision.HIGHEST, preferred_element_type=_F32)
    c = c + carry_ref[...]
    c_ref[...] = c
    ct_ref[...] = c.T[:FOX_HEADS, :]
    carry_ref[...] = c[tm - 1:tm, :]


def _forget_cumsum(u, w_in_t, f_col, bf, init, nb, n_valid, tm):
    m, k = u.shape
    nt = m // (nb * tm)
    f_blk = f_col // LANES
    assert f_blk * LANES == f_col
    return pl.pallas_call(
        functools.partial(_cumsum_kernel, n_valid=n_valid),
        grid=(nb, nt),
        in_specs=[pl.BlockSpec((tm, k), lambda b, t: (b * nt + t, 0)),
                  pl.BlockSpec((LANES, k), lambda b, t: (f_blk, 0)),
                  pl.BlockSpec((1, LANES), lambda b, t: (0, 0)),
                  pl.BlockSpec((1, LANES), lambda b, t: (0, 0))],
        out_specs=[pl.BlockSpec((tm, LANES), lambda b, t: (b * nt + t, 0)),
                   pl.BlockSpec((FOX_HEADS, tm), lambda b, t: (0, b * nt + t))],
        out_shape=[jax.ShapeDtypeStruct((m, LANES), _F32),
                   jax.ShapeDtypeStruct((FOX_HEADS, m), _F32)],
        scratch_shapes=[pltpu.VMEM((1, LANES), _F32)],
        compiler_params=_params("arbitrary", "arbitrary"),
        name="forget_cumsum",
    )(u, w_in_t, bf, init)


def _dot_tn(a, b):
    return lax.dot_general(a, b, (((0,), (0,)), ((), ())), preferred_element_type=_F32)


def _lane_tile(x, n):
    return jnp.concatenate([x] * (n // LANES), axis=1)


def _flash_step(t, v, rq, m, l, acc):
    m_new = jnp.maximum(m, jnp.max(t, axis=0, keepdims=True) + rq)
    alpha = jnp.exp2(m - m_new)
    p = jnp.exp2(t - (m_new - rq))
    l = alpha * l + jnp.sum(p, axis=0, keepdims=True)
    acc = alpha * acc + _dot_tn(v, p.astype(v.dtype))
    return m_new, l, acc


def _flash_init(dv, tq):
    return (jnp.full((1, tq), _NEG_INF, _F32), jnp.zeros((1, tq), _F32), jnp.zeros((dv, tq), _F32))


def _block_start(j, size):
    return j * size if isinstance(j, int) else pl.multiple_of(j * size, size)


class _QueryTile:
    def __init__(self, streams, dv, tq, rq, first_scores, first_values, scores, values,
                 first_ref, t_ref):
        self.streams, self.dv, self.tq, self.rq = streams, dv, tq, rq
        self.first_scores, self.first_values = first_scores, first_values
        self.scores, self.values, self.first_ref, self.t_ref = scores, values, first_ref, t_ref

    def stage_first(self):
        for s in range(self.streams):
            t = self.first_scores(s)
            self.first_ref[s, :t.shape[0], :] = t
        self.first_keys = t.shape[0]

    def start(self, n_pairs):
        if n_pairs:
            for s in range(self.streams):
                self.t_ref[0, s] = self.scores(s, 0)
        state = []
        for s in range(self.streams):
            state.extend(_flash_step(self.first_ref[s, :self.first_keys, :], self.first_values(s),
                                     self.rq[s], *_flash_init(self.dv, self.tq)))
        return tuple(state)

    def run_pairs(self, n_pairs, state):
        def half(st, slot, blk, nxt):
            for s in range(self.streams):
                self.t_ref[1 - slot, s] = self.scores(s, nxt)
            out = []
            for s in range(self.streams):
                out.extend(_flash_step(self.t_ref[slot, s], self.values(s, blk), self.rq[s],
                                       *st[3 * s:3 * s + 3]))
            return tuple(out)

        def pair(j, st):
            st = half(st, 0, 2 * j, 2 * j + 1)
            return half(st, 1, 2 * j + 1, jnp.minimum(2 * j + 2, 2 * n_pairs - 1))

        return lax.fori_loop(0, n_pairs, pair, state) if n_pairs else state


def _sweep_tile_pairs(n_tiles, make_tile, finish):
    def pair_tiles(a):
        tiles = [make_tile(2 * a, 0, None, a % 2), make_tile(2 * a + 1, 1, 2 * a, a % 2)]
        for t in tiles:
            t.stage_first()
        return tiles

    tiles = pair_tiles(0)
    for a in range(n_tiles // 2):
        nxt = pair_tiles(a + 1) if a + 1 < n_tiles // 2 else None
        states = [t.start(a) for t in tiles]
        states = [t.run_pairs(a, st) for t, st in zip(tiles, states)]
        for t, st in zip(tiles, states):
            finish(t, st)
        tiles = nxt


def _fox_kernel(q_ref, k_ref, v_ref, km_ref, vm_ref, c_ref, cm_ref, ct_ref, o_ref, colb_ref,
                colbm_ref, t_ref, first_even_ref, first_odd_ref, *, tq, hps):
    nq = q_ref.shape[0] // tq
    d = HEAD_DIM
    heads = [pl.program_id(1) * hps + hh for hh in range(hps)]
    lanes = lambda hh: slice(hh * d, (hh + 1) * d)

    lane = lax.broadcasted_iota(jnp.int32, (tq, LANES), 1)
    lane_m = lax.broadcasted_iota(jnp.int32, (META_ROWS, LANES), 1)
    for hh in range(hps):
        def fill(i, carry, hh=hh):
            rs = pl.multiple_of(i * tq, tq)
            ck = jnp.sum(jnp.where(lane == heads[hh], c_ref[pl.ds(rs, tq), :], 0.0),
                         axis=1, keepdims=True)
            colb_ref[hh, pl.ds(rs, tq), :] = jnp.broadcast_to(-LOG2E * ck, (tq, LANES))
            return carry

        lax.fori_loop(0, nq, fill, 0)
        ckm = jnp.sum(jnp.where(lane_m == heads[hh], cm_ref[...], 0.0), axis=1, keepdims=True)
        colbm_ref[hh] = jnp.broadcast_to(-LOG2E * ckm, (META_ROWS, LANES))

    meta_valid = lax.broadcasted_iota(jnp.int32, (META_ROWS, tq), 0) < N_META
    causal = (lax.broadcasted_iota(jnp.int32, (tq, tq), 1)
              >= lax.broadcasted_iota(jnp.int32, (tq, tq), 0))

    def make_tile(qi, slot, odd_block, stage_slot):
        qs = qi * tq
        q = [q_ref[pl.ds(qs, tq), lanes(hh)] for hh in range(hps)]
        rq = [LOG2E * ct_ref[hh, qi] for hh in range(hps)]

        def scores(hh, j):
            ks = _block_start(j, tq)
            return (_dot_nt(k_ref[pl.ds(ks, tq), lanes(hh)], q[hh])
                    + _lane_tile(colb_ref[hh, pl.ds(ks, tq), :], tq))

        def values(hh, j):
            return v_ref[pl.ds(_block_start(j, tq), tq), lanes(hh)]

        def first_scores(hh):
            tm = _dot_nt(km_ref[:, lanes(hh)], q[hh]) + _lane_tile(colbm_ref[hh], tq)
            ts = [jnp.where(causal, scores(hh, qi), _NEG_INF), jnp.where(meta_valid, tm, _NEG_INF)]
            if odd_block is not None:
                ts.append(scores(hh, odd_block))
            return jnp.concatenate(ts, axis=0)

        def first_values(hh):
            vs = [values(hh, qi), vm_ref[:, lanes(hh)]]
            if odd_block is not None:
                vs.append(values(hh, odd_block))
            return jnp.concatenate(vs, axis=0)

        first_ref = (first_even_ref if odd_block is None else first_odd_ref).at[stage_slot]
        tile = _QueryTile(hps, d, tq, rq, first_scores, first_values, scores, values, first_ref,
                          t_ref.at[slot])
        tile.qs = qs
        return tile

    def finish(tile, state):
        for hh in range(hps):
            _, l, acc = state[3 * hh:3 * hh + 3]
            o_ref[pl.ds(tile.qs, tq), lanes(hh)] = (acc / l).T.astype(o_ref.dtype)

    _sweep_tile_pairs(nq, make_tile, finish)


def _fox_attention(proj, proj_meta, c, cm, ct, nb, seq, tq=ATT_TILE, hps=FOX_HEADS_PER_STEP):
    m = proj.shape[0]
    w = hps * HEAD_DIM
    qb, kb, vb = (i * FOX_HEADS // hps for i in range(3))
    nq = seq // tq
    tok = lambda off: pl.BlockSpec((seq, w), lambda b, s: (b, off + s))
    meta = lambda off: pl.BlockSpec((META_ROWS, w), lambda b, s: (0, off + s))
    return pl.pallas_call(
        functools.partial(_fox_kernel, tq=tq, hps=hps),
        grid=(nb, FOX_HEADS // hps),
        in_specs=[tok(qb), tok(kb), tok(vb), meta(kb), meta(vb),
                  pl.BlockSpec((seq, LANES), lambda b, s: (b, 0), pipeline_mode=pl.Buffered(1)),
                  pl.BlockSpec((META_ROWS, LANES), lambda b, s: (0, 0)),
                  pl.BlockSpec((hps, nq, 1, tq), lambda b, s: (s, b, 0, 0))],
        out_specs=pl.BlockSpec((seq, w), lambda b, s: (b, s)),
        out_shape=jax.ShapeDtypeStruct((m, FOX_HEADS * HEAD_DIM), _BF16),
        scratch_shapes=[pltpu.VMEM((hps, seq, LANES), _F32),
                        pltpu.VMEM((hps, META_ROWS, LANES), _F32),
                        pltpu.VMEM((2, 2, hps, tq, tq), _F32),
                        pltpu.VMEM((2, hps, tq + META_ROWS, tq), _F32),
                        pltpu.VMEM((2, hps, 2 * tq + META_ROWS, tq), _F32)],
        compiler_params=_params("parallel", "arbitrary"),
        name="fox_attention",
    )(proj, proj, proj, proj_meta, proj_meta, c, cm, ct)


def _diff_kernel(q_ref, k_ref, v_ref, km_ref, vm_ref, slope_ref, lq1_ref, lk1_ref, lq2_ref,
                 lk2_ref, g_ref, o_ref, colb_ref, t_ref, first_even_ref, first_odd_ref, *, tq,
                 lambda_init):
    nq = q_ref.shape[0] // tq
    d = HEAD_DIM
    comp = lambda c: slice(c * d, (c + 1) * d)
    slope2 = LOG2E * slope_ref[:, 0:1]

    key_row = lax.broadcasted_iota(jnp.int32, (tq, LANES), 0)

    def fill(i, carry):
        rs = pl.multiple_of(i * tq, tq)
        colb_ref[pl.ds(rs, tq), :] = slope2 * (key_row + (rs + N_META)).astype(_F32)
        return carry

    lax.fori_loop(0, nq, fill, 0)
    colbm = slope2 * lax.broadcasted_iota(jnp.int32, (META_ROWS, LANES), 0).astype(_F32)
    query_col = lax.broadcasted_iota(jnp.int32, (1, tq), 1)

    meta_valid = lax.broadcasted_iota(jnp.int32, (META_ROWS, tq), 0) < N_META
    causal = (lax.broadcasted_iota(jnp.int32, (tq, tq), 1)
              >= lax.broadcasted_iota(jnp.int32, (tq, tq), 0))
    lam = (jnp.exp(jnp.sum(lq1_ref[...] * lk1_ref[...], axis=1, keepdims=True))
           - jnp.exp(jnp.sum(lq2_ref[...] * lk2_ref[...], axis=1, keepdims=True)) + lambda_init)

    def make_tile(qi, slot, odd_block, stage_slot):
        qs = qi * tq
        q = [q_ref[pl.ds(qs, tq), comp(c)] for c in range(2)]
        rq = -slope2 * (query_col + (qs + N_META)).astype(_F32)

        def scores(c, j):
            ks = _block_start(j, tq)
            return (_dot_nt(k_ref[pl.ds(ks, tq), comp(c)], q[c])
                    + _lane_tile(colb_ref[pl.ds(ks, tq), :], tq))

        def values(c, j):
            return v_ref[pl.ds(_block_start(j, tq), tq), :]

        def first_scores(c):
            tm = _dot_nt(km_ref[:, comp(c)], q[c]) + _lane_tile(colbm, tq)
            ts = [jnp.where(causal, scores(c, qi), _NEG_INF), jnp.where(meta_valid, tm, _NEG_INF)]
            if odd_block is not None:
                ts.append(scores(c, odd_block))
            return jnp.concatenate(ts, axis=0)

        def first_values(c):
            vs = [values(c, qi), vm_ref[...]]
            if odd_block is not None:
                vs.append(values(c, odd_block))
            return jnp.concatenate(vs, axis=0)

        first_ref = (first_even_ref if odd_block is None else first_odd_ref).at[stage_slot]
        tile = _QueryTile(2, 2 * d, tq, (rq, rq), first_scores, first_values, scores, values,
                          first_ref, t_ref.at[slot])
        tile.qs = qs
        return tile

    def finish(tile, state):
        _, l1, a1, _, l2, a2 = state
        o = (a1 / l1 - lam * (a2 / l2)).T
        y = _rms(o, g_ref[...]) * (1.0 - lambda_init)
        o_ref[pl.ds(tile.qs, tq), :] = y.astype(o_ref.dtype)

    _sweep_tile_pairs(nq, make_tile, finish)


def _diff_attention(proj, proj_meta, slopes, lq1, lk1, lq2, lk2, subln_g, lambda_init, nb, seq,
                    tq=ATT_TILE):
    m = proj.shape[0]
    w = 2 * HEAD_DIM
    qb, kb, vb = (i * DIFF_HEADS for i in range(3))
    tok = lambda off: pl.BlockSpec((seq, w), lambda b, h: (b, off + h))
    meta = lambda off: pl.BlockSpec((META_ROWS, w), lambda b, h: (0, off + h))
    vec = lambda n: pl.BlockSpec((1, n), lambda b, h: (0, 0))
    return pl.pallas_call(
        functools.partial(_diff_kernel, tq=tq, lambda_init=lambda_init),
        grid=(nb, DIFF_HEADS),
        in_specs=[tok(qb), tok(kb), tok(vb), meta(kb), meta(vb),
                  pl.BlockSpec((None, 1, LANES), lambda b, h: (h, 0, 0)),
                  vec(HEAD_DIM), vec(HEAD_DIM), vec(HEAD_DIM), vec(HEAD_DIM), vec(w)],
        out_specs=pl.BlockSpec((seq, w), lambda b, h: (b, h)),
        out_shape=jax.ShapeDtypeStruct((m, DIFF_HEADS * w), _BF16),
        scratch_shapes=[pltpu.VMEM((seq, LANES), _F32), pltpu.VMEM((2, 2, 2, tq, tq), _F32),
                        pltpu.VMEM((2, 2, tq + META_ROWS, tq), _F32),
                        pltpu.VMEM((2, 2, 2 * tq + META_ROWS, tq), _F32)],
        compiler_params=_params("parallel", "arbitrary"),
        name="diff_attention",
    )(proj, proj, proj, proj_meta, proj_meta, slopes, lq1, lk1, lq2, lk2, subln_g)


def _ffn(xn, xn_meta, w_gate, w_up, w_down):
    hid, hid_meta = _gateup(xn, xn_meta, w_gate, w_up)
    return _mm(hid, hid_meta, w_down, SUBLAYER_OUT_DTYPE, tm=512, tn=512)


def kernel(x, meta_tokens, ff1_pre_g, ff1_w_gate, ff1_w_up, ff1_w_down, ff1_post_g, mix_pre_g, w_in, b_forget, lambda_q1, lambda_k1, lambda_q2, lambda_k2, diff_subln_g, w_o_fox, w_o_diff, w_out, mix_post_g, ff2_pre_g, ff2_w_gate, ff2_w_up, ff2_w_down, ff2_post_g):
    nb, seq, d_model = x.shape
    depth = w_in.shape[0]
    assert depth == 1, "deeper stacks need the meta-token query path"
    h = x.reshape(nb * seq, d_model)
    hm = jnp.pad(meta_tokens.astype(x.dtype), ((0, META_ROWS - N_META), (0, 0)))
    slopes = jnp.exp2(-ALIBI_MAX_BIAS * jnp.arange(1, DIFF_HEADS + 1, dtype=_F32) / DIFF_HEADS)
    slopes = jnp.broadcast_to(slopes[:, None, None], (DIFF_HEADS, 1, LANES))
    nq = seq // ATT_TILE
    fox_w = FOX_HEADS * HEAD_DIM
    diff_w = DIFF_HEADS * 2 * HEAD_DIM
    f_col = 3 * fox_w
    vec = lambda a: a.reshape(1, -1)
    bf = lambda a: a.astype(_BF16)

    for layer in range(depth):
        lambda_init = 0.8 - 0.6 * math.exp(-0.3 * layer)

        wg, wu, wd = ff1_w_gate[layer], ff1_w_up[layer], bf(ff1_w_down[layer])
        pre, post, mix_pre = vec(ff1_pre_g[layer]), vec(ff1_post_g[layer]), vec(mix_pre_g[layer])
        y, ym = _ffn(_norm(h, pre), _norm(hm, pre), wg, wu, wd)
        h, u = _resid_norm(h, y, post, mix_pre, 0.5)
        hm, um = _resid_norm(hm, ym, post, mix_pre, 0.5)

        w = w_in[layer].T
        d_col = f_col + FOX_HEADS
        proj_f, proj_fm = _proj(u, um, w, 0, 3 * fox_w, fox_w)
        proj_d, proj_dm = _proj(u, um, w, d_col, 3 * diff_w, diff_w)
        gates, _ = _proj(u, None, w, d_col + 3 * diff_w, 2 * d_model, None)

        b_f = jnp.pad(b_forget[layer].astype(_F32), (0, LANES - FOX_HEADS)).reshape(1, LANES)
        cm, _ = _forget_cumsum(um, w, f_col, b_f, jnp.zeros((1, LANES), _F32), 1, N_META, META_ROWS)
        c, ct = _forget_cumsum(u, w, f_col, b_f, cm[META_ROWS - 1:], nb, CUMSUM_TILE, CUMSUM_TILE)
        ct = ct.reshape(FOX_HEADS, nb * nq, 1, ATT_TILE)

        fox = _fox_attention(proj_f, proj_fm, c, cm, ct, nb, seq)
        diff = _diff_attention(proj_d, proj_dm, slopes, vec(lambda_q1[layer]).astype(_F32),
                               vec(lambda_k1[layer]).astype(_F32), vec(lambda_q2[layer]).astype(_F32),
                               vec(lambda_k2[layer]).astype(_F32), vec(diff_subln_g[layer]).astype(_F32),
                               lambda_init, nb, seq)
        merged = _merge(fox, diff, w_o_fox[layer], w_o_diff[layer], gates, 0, d_model)
        mix, _ = _mm(merged, None, w_out[layer], SUBLAYER_OUT_DTYPE, tm=1024, tn=512)

        wg, wu, wd = ff2_w_gate[layer], ff2_w_up[layer], bf(ff2_w_down[layer])
        pre2, post2 = vec(ff2_pre_g[layer]), vec(ff2_post_g[layer])
        h, xn = _resid_norm(h, mix, vec(mix_post_g[layer]), pre2, 1.0)
        y, _ = _ffn(xn, None, wg, wu, wd)
        h = _resid(h, y, post2, 0.5)

    return h.reshape(nb, seq, d_model)
```

```python
import functools
import math

import jax
import jax.numpy as jnp
from jax import lax
from jax.experimental import pallas as pl
from jax.experimental.pallas import tpu as pltpu

N_META = 16
FOX_HEADS = 16
DIFF_HEADS = 8
HEAD_DIM = 128
RMS_EPS = 1e-6
ALIBI_MAX_BIAS = 8.0

LANES = 128
F32_SUBLANES = 8
META_ROWS = 128
VMEM_LIMIT = 58 * 1024 * 1024
ATT_TILE = 512
FOX_HEADS_PER_STEP = 2
LOG2E = math.log2(math.e)
CUMSUM_TILE = 256
NORM_ROWS = 512

_F32 = jnp.float32
_BF16 = jnp.bfloat16
_NEG_INF = float("-inf")
SUBLAYER_OUT_DTYPE = _BF16


def _params(*sem):
    return pltpu.CompilerParams(dimension_semantics=sem, vmem_limit_bytes=VMEM_LIMIT)


def _row_tile(m, want):
    return want if m % want == 0 else m


def _rms(x, g):
    return x * lax.rsqrt(jnp.mean(x * x, axis=-1, keepdims=True) + RMS_EPS) * g


def _sigmoid(x):
    return 0.5 * jnp.tanh(0.5 * x) + 0.5


def _dot(a, b):
    return jnp.dot(a, b, preferred_element_type=_F32)


def _dot_nt(a, b):
    return lax.dot_general(a, b, (((1,), (1,)), ((), ())), preferred_element_type=_F32)


def _norm_kernel(h_ref, g_ref, o_ref):
    o_ref[...] = _rms(h_ref[...], g_ref[...]).astype(o_ref.dtype)


def _norm(h, g, tm=NORM_ROWS):
    m, d = h.shape
    tm = _row_tile(m, tm)
    return pl.pallas_call(
        _norm_kernel,
        grid=(m // tm,),
        in_specs=[pl.BlockSpec((tm, d), lambda i: (i, 0)),
                  pl.BlockSpec((1, d), lambda i: (0, 0))],
        out_specs=pl.BlockSpec((tm, d), lambda i: (i, 0)),
        out_shape=jax.ShapeDtypeStruct((m, d), _BF16),
        compiler_params=_params("parallel"),
        name="norm",
    )(h, g)


def _resid_norm_kernel(h_ref, y_ref, gpost_ref, gpre_ref, hn_ref, xn_ref, *, alpha):
    hn = h_ref[...] + alpha * _rms(y_ref[...].astype(_F32), gpost_ref[...])
    hn_ref[...] = hn
    xn_ref[...] = _rms(hn, gpre_ref[...]).astype(xn_ref.dtype)


def _resid_norm(h, y, gpost, gpre, alpha, tm=NORM_ROWS):
    m, d = h.shape
    tm = _row_tile(m, tm)
    row = pl.BlockSpec((tm, d), lambda i: (i, 0))
    vec = pl.BlockSpec((1, d), lambda i: (0, 0))
    return pl.pallas_call(
        functools.partial(_resid_norm_kernel, alpha=alpha),
        grid=(m // tm,),
        in_specs=[row, row, vec, vec],
        out_specs=[row, row],
        out_shape=[jax.ShapeDtypeStruct((m, d), _F32), jax.ShapeDtypeStruct((m, d), _BF16)],
        compiler_params=_params("parallel"),
        name="resid_norm",
    )(h, y, gpost, gpre)


def _resid_kernel(h_ref, y_ref, gpost_ref, o_ref, *, alpha):
    o_ref[...] = h_ref[...] + alpha * _rms(y_ref[...].astype(_F32), gpost_ref[...])


def _resid(h, y, gpost, alpha, tm=NORM_ROWS):
    m, d = h.shape
    tm = _row_tile(m, tm)
    row = pl.BlockSpec((tm, d), lambda i: (i, 0))
    vec = pl.BlockSpec((1, d), lambda i: (0, 0))
    return pl.pallas_call(
        functools.partial(_resid_kernel, alpha=alpha),
        grid=(m // tm,),
        in_specs=[row, row, vec],
        out_specs=row,
        out_shape=jax.ShapeDtypeStruct((m, d), _F32),
        compiler_params=_params("parallel"),
        name="resid",
    )(h, y, gpost)


def _mm_kernel(a_ref, w_ref, o_ref):
    o_ref[...] = _dot(a_ref[...], w_ref[...].astype(a_ref.dtype)).astype(o_ref.dtype)


def _mm(a, w, out_dtype, tm, tn):
    m, k = a.shape
    n = w.shape[1]
    tm = _row_tile(m, tm)
    return pl.pallas_call(
        _mm_kernel,
        grid=(m // tm, n // tn),
        in_specs=[pl.BlockSpec((tm, k), lambda i, j: (i, 0)),
                  pl.BlockSpec((k, tn), lambda i, j: (0, j))],
        out_specs=pl.BlockSpec((tm, tn), lambda i, j: (i, j)),
        out_shape=jax.ShapeDtypeStruct((m, n), out_dtype),
        compiler_params=_params("parallel", "arbitrary"),
        name="mm",
    )(a, w)


def _gateup_kernel(x_ref, wg_ref, wu_ref, o_ref):
    x = x_ref[...]
    g = _dot(x, wg_ref[...].astype(x.dtype))
    u = _dot(x, wu_ref[...].astype(x.dtype))
    o_ref[...] = (g * _sigmoid(g) * u).astype(o_ref.dtype)


def _gateup(x, wg, wu, tm=2048, tn=256):
    m, k = x.shape
    n = wg.shape[1]
    tm = _row_tile(m, tm)
    wspec = pl.BlockSpec((k, tn), lambda i, j: (0, j))
    return pl.pallas_call(
        _gateup_kernel,
        grid=(m // tm, n // tn),
        in_specs=[pl.BlockSpec((tm, k), lambda i, j: (i, 0), pipeline_mode=pl.Buffered(1)),
                  wspec, wspec],
        out_specs=pl.BlockSpec((tm, tn), lambda i, j: (i, j)),
        out_shape=jax.ShapeDtypeStruct((m, n), _BF16),
        compiler_params=_params("parallel", "arbitrary"),
        name="gateup",
    )(x, wg, wu)


def _proj_kernel(u_ref, wt_ref, o_ref, *, tn, q_cols, q_scale):
    acc = _dot_nt(u_ref[...], wt_ref[...].astype(u_ref.dtype))
    if q_cols is None:
        o_ref[...] = _sigmoid(acc).astype(o_ref.dtype)
    else:
        scale = jnp.where(pl.program_id(1) * tn < q_cols, q_scale, 1.0).astype(_F32)
        o_ref[...] = (acc * scale).astype(o_ref.dtype)


def _proj(u, w_in_t, col0, n_out, q_cols, tm=1024, tn=512):
    m, k = u.shape
    tm = _row_tile(m, tm)
    return pl.pallas_call(
        functools.partial(_proj_kernel, tn=tn, q_cols=q_cols, q_scale=HEAD_DIM ** -0.5 * LOG2E),
        grid=(m // tm, n_out // tn),
        in_specs=[pl.BlockSpec((tm, k), lambda i, j: (i, 0)),
                  pl.BlockSpec((pl.Element(tn), pl.Element(k)),
                               lambda i, j: (pl.multiple_of(col0 + j * tn, F32_SUBLANES), 0))],
        out_specs=pl.BlockSpec((tm, tn), lambda i, j: (i, j)),
        out_shape=jax.ShapeDtypeStruct((m, n_out), _BF16),
        compiler_params=_params("parallel", "arbitrary"),
        name="proj_gate" if q_cols is None else "proj_qkv",
    )(u, w_in_t)


def _merge_kernel(af_ref, ad_ref, wf_ref, wd_ref, gf_ref, gd_ref, o_ref):
    yf = _dot(af_ref[...], wf_ref[...].astype(af_ref.dtype))
    yd = _dot(ad_ref[...], wd_ref[...].astype(ad_ref.dtype))
    o_ref[...] = (gf_ref[...].astype(_F32) * yf + gd_ref[...].astype(_F32) * yd).astype(o_ref.dtype)


def _merge(fox, diff, w_of, w_od, gates, gf_col, gd_col, tm=1024, tn=512):
    m, k = fox.shape
    n = w_of.shape[1]
    aspec = pl.BlockSpec((tm, k), lambda i, j: (i, 0))
    wspec = pl.BlockSpec((k, tn), lambda i, j: (0, j))
    return pl.pallas_call(
        _merge_kernel,
        grid=(m // tm, n // tn),
        in_specs=[aspec, aspec, wspec, wspec,
                  pl.BlockSpec((tm, tn), lambda i, j: (i, gf_col // tn + j)),
                  pl.BlockSpec((tm, tn), lambda i, j: (i, gd_col // tn + j))],
        out_specs=pl.BlockSpec((tm, tn), lambda i, j: (i, j)),
        out_shape=jax.ShapeDtypeStruct((m, n), _BF16),
        compiler_params=_params("parallel", "arbitrary"),
        name="merge",
    )(fox, diff, w_of, w_od, gates, gates)


def _cumsum_kernel(u_ref, wf_ref, bf_ref, init_ref, c_ref, ct_ref, carry_ref, *, n_valid):
    tm = u_ref.shape[0]

    @pl.when(pl.program_id(1) == 0)
    def _():
        carry_ref[...] = init_ref[...]

    is_head = lax.broadcasted_iota(jnp.int32, wf_ref.shape, 0) < FOX_HEADS
    wf = jnp.where(is_head, wf_ref[...], 0.0).astype(u_ref.dtype)
    x = _dot_nt(u_ref[...], wf) + bf_ref[...]
    logf = jnp.minimum(x, 0.0) - jnp.log1p(jnp.exp(-jnp.abs(x)))
    row = lax.broadcasted_iota(jnp.int32, (tm, tm), 0)
    col = lax.broadcasted_iota(jnp.int32, (tm, tm), 1)
    if n_valid < tm:
        logf = jnp.where(lax.broadcasted_iota(jnp.int32, (tm, LANES), 0) < n_valid, logf, 0.0)
    tri = (row >= col).astype(_F32)
    c = jnp.dot(tri, logf, precision=lax.Precision.HIGHEST, preferred_element_type=_F32)
    c = c + carry_ref[...]
    c_ref[...] = c
    ct_ref[...] = c.T[:FOX_HEADS, :]
    carry_ref[...] = c[tm - 1:tm, :]


def _forget_cumsum(u, w_in_t, f_col, bf, init, nb, n_valid, tm):
    m, k = u.shape
    nt = m // (nb * tm)
    f_blk = f_col // LANES
    assert f_blk * LANES == f_col
    return pl.pallas_call(
        functools.partial(_cumsum_kernel, n_valid=n_valid),
        grid=(nb, nt),
        in_specs=[pl.BlockSpec((tm, k), lambda b, t: (b * nt + t, 0)),
                  pl.BlockSpec((LANES, k), lambda b, t: (f_blk, 0)),
                  pl.BlockSpec((1, LANES), lambda b, t: (0, 0)),
                  pl.BlockSpec((1, LANES), lambda b, t: (0, 0))],
        out_specs=[pl.BlockSpec((tm, LANES), lambda b, t: (b * nt + t, 0)),
                   pl.BlockSpec((FOX_HEADS, tm), lambda b, t: (0, b * nt + t))],
        out_shape=[jax.ShapeDtypeStruct((m, LANES), _F32),
                   jax.ShapeDtypeStruct((FOX_HEADS, m), _F32)],
        scratch_shapes=[pltpu.VMEM((1, LANES), _F32)],
        compiler_params=_params("arbitrary", "arbitrary"),
        name="forget_cumsum",
    )(u, w_in_t, bf, init)


def _dot_tn(a, b):
    return lax.dot_general(a, b, (((0,), (0,)), ((), ())), preferred_element_type=_F32)


def _lane_tile(x, n):
    return jnp.concatenate([x] * (n // LANES), axis=1)


def _flash_step(t, v, rq, m, l, acc):
    m_new = jnp.maximum(m, jnp.max(t, axis=0, keepdims=True) + rq)
    alpha = jnp.exp2(m - m_new)
    p = jnp.exp2(t - (m_new - rq))
    l = alpha * l + jnp.sum(p, axis=0, keepdims=True)
    acc = alpha * acc + _dot_tn(v, p.astype(v.dtype))
    return m_new, l, acc


def _flash_init(dv, tq):
    return (jnp.full((1, tq), _NEG_INF, _F32), jnp.zeros((1, tq), _F32), jnp.zeros((dv, tq), _F32))


def _block_start(j, size):
    return j * size if isinstance(j, int) else pl.multiple_of(j * size, size)


class _QueryTile:
    def __init__(self, streams, dv, tq, rq, first_scores, first_values, scores, values,
                 first_ref, t_ref):
        self.streams, self.dv, self.tq, self.rq = streams, dv, tq, rq
        self.first_scores, self.first_values = first_scores, first_values
        self.scores, self.values, self.first_ref, self.t_ref = scores, values, first_ref, t_ref

    def stage_first(self):
        for s in range(self.streams):
            t = self.first_scores(s)
            self.first_ref[s, :t.shape[0], :] = t
        self.first_keys = t.shape[0]

    def start(self, n_pairs):
        if n_pairs:
            for s in range(self.streams):
                self.t_ref[0, s] = self.scores(s, 0)
        state = []
        for s in range(self.streams):
            state.extend(_flash_step(self.first_ref[s, :self.first_keys, :], self.first_values(s),
                                     self.rq[s], *_flash_init(self.dv, self.tq)))
        return tuple(state)

    def run_pairs(self, n_pairs, state):
        def half(st, slot, blk, nxt):
            for s in range(self.streams):
                self.t_ref[1 - slot, s] = self.scores(s, nxt)
            out = []
            for s in range(self.streams):
                out.extend(_flash_step(self.t_ref[slot, s], self.values(s, blk), self.rq[s],
                                       *st[3 * s:3 * s + 3]))
            return tuple(out)

        def pair(j, st):
            st = half(st, 0, 2 * j, 2 * j + 1)
            return half(st, 1, 2 * j + 1, jnp.minimum(2 * j + 2, 2 * n_pairs - 1))

        return lax.fori_loop(0, n_pairs, pair, state) if n_pairs else state


def _sweep_tile_pairs(n_tiles, make_tile, finish):
    def pair_tiles(a):
        tiles = [make_tile(2 * a, 0, None, a % 2), make_tile(2 * a + 1, 1, 2 * a, a % 2)]
        for t in tiles:
            t.stage_first()
        return tiles

    tiles = pair_tiles(0)
    for a in range(n_tiles // 2):
        nxt = pair_tiles(a + 1) if a + 1 < n_tiles // 2 else None
        states = [t.start(a) for t in tiles]
        states = [t.run_pairs(a, st) for t, st in zip(tiles, states)]
        for t, st in zip(tiles, states):
            finish(t, st)
        tiles = nxt


def _fox_kernel(q_ref, k_ref, v_ref, km_ref, vm_ref, c_ref, cm_ref, ct_ref, o_ref, colb_ref,
                colbm_ref, t_ref, first_even_ref, first_odd_ref, *, tq, hps):
    nq = q_ref.shape[0] // tq
    d = HEAD_DIM
    heads = [pl.program_id(1) * hps + hh for hh in range(hps)]
    lanes = lambda hh: slice(hh * d, (hh + 1) * d)

    lane = lax.broadcasted_iota(jnp.int32, (tq, LANES), 1)
    lane_m = lax.broadcasted_iota(jnp.int32, (META_ROWS, LANES), 1)
    for hh in range(hps):
        def fill(i, carry, hh=hh):
            rs = pl.multiple_of(i * tq, tq)
            ck = jnp.sum(jnp.where(lane == heads[hh], c_ref[pl.ds(rs, tq), :], 0.0),
                         axis=1, keepdims=True)
            colb_ref[hh, pl.ds(rs, tq), :] = jnp.broadcast_to(-LOG2E * ck, (tq, LANES))
            return carry

        lax.fori_loop(0, nq, fill, 0)
        ckm = jnp.sum(jnp.where(lane_m == heads[hh], cm_ref[...], 0.0), axis=1, keepdims=True)
        colbm_ref[hh] = jnp.broadcast_to(-LOG2E * ckm, (META_ROWS, LANES))

    meta_valid = lax.broadcasted_iota(jnp.int32, (META_ROWS, tq), 0) < N_META
    causal = (lax.broadcasted_iota(jnp.int32, (tq, tq), 1)
              >= lax.broadcasted_iota(jnp.int32, (tq, tq), 0))

    def make_tile(qi, slot, odd_block, stage_slot):
        qs = qi * tq
        q = [q_ref[pl.ds(qs, tq), lanes(hh)] for hh in range(hps)]
        rq = [LOG2E * ct_ref[hh, qi] for hh in range(hps)]

        def scores(hh, j):
            ks = _block_start(j, tq)
            return (_dot_nt(k_ref[pl.ds(ks, tq), lanes(hh)], q[hh])
                    + _lane_tile(colb_ref[hh, pl.ds(ks, tq), :], tq))

        def values(hh, j):
            return v_ref[pl.ds(_block_start(j, tq), tq), lanes(hh)]

        def first_scores(hh):
            tm = _dot_nt(km_ref[:, lanes(hh)], q[hh]) + _lane_tile(colbm_ref[hh], tq)
            ts = [jnp.where(causal, scores(hh, qi), _NEG_INF), jnp.where(meta_valid, tm, _NEG_INF)]
            if odd_block is not None:
                ts.append(scores(hh, odd_block))
            return jnp.concatenate(ts, axis=0)

        def first_values(hh):
            vs = [values(hh, qi), vm_ref[:, lanes(hh)]]
            if odd_block is not None:
                vs.append(values(hh, odd_block))
            return jnp.concatenate(vs, axis=0)

        first_ref = (first_even_ref if odd_block is None else first_odd_ref).at[stage_slot]
        tile = _QueryTile(hps, d, tq, rq, first_scores, first_values, scores, values, first_ref,
                          t_ref.at[slot])
        tile.qs = qs
        return tile

    def finish(tile, state):
        for hh in range(hps):
            _, l, acc = state[3 * hh:3 * hh + 3]
            o_ref[pl.ds(tile.qs, tq), lanes(hh)] = (acc / l).T.astype(o_ref.dtype)

    _sweep_tile_pairs(nq, make_tile, finish)


def _fox_attention(proj, proj_meta, c, cm, ct, nb, seq, tq=ATT_TILE, hps=FOX_HEADS_PER_STEP):
    m = proj.shape[0]
    w = hps * HEAD_DIM
    qb, kb, vb = (i * FOX_HEADS // hps for i in range(3))
    nq = seq // tq
    tok = lambda off: pl.BlockSpec((seq, w), lambda b, s: (b, off + s))
    meta = lambda off: pl.BlockSpec((META_ROWS, w), lambda b, s: (0, off + s))
    return pl.pallas_call(
        functools.partial(_fox_kernel, tq=tq, hps=hps),
        grid=(nb, FOX_HEADS // hps),
        in_specs=[tok(qb), tok(kb), tok(vb), meta(kb), meta(vb),
                  pl.BlockSpec((seq, LANES), lambda b, s: (b, 0), pipeline_mode=pl.Buffered(1)),
                  pl.BlockSpec((META_ROWS, LANES), lambda b, s: (0, 0)),
                  pl.BlockSpec((hps, nq, 1, tq), lambda b, s: (s, b, 0, 0))],
        out_specs=pl.BlockSpec((seq, w), lambda b, s: (b, s)),
        out_shape=jax.ShapeDtypeStruct((m, FOX_HEADS * HEAD_DIM), _BF16),
        scratch_shapes=[pltpu.VMEM((hps, seq, LANES), _F32),
                        pltpu.VMEM((hps, META_ROWS, LANES), _F32),
                        pltpu.VMEM((2, 2, hps, tq, tq), _F32),
                        pltpu.VMEM((2, hps, tq + META_ROWS, tq), _F32),
                        pltpu.VMEM((2, hps, 2 * tq + META_ROWS, tq), _F32)],
        compiler_params=_params("parallel", "arbitrary"),
        name="fox_attention",
    )(proj, proj, proj, proj_meta, proj_meta, c, cm, ct)


def _diff_kernel(q_ref, k_ref, v_ref, km_ref, vm_ref, slope_ref, lq1_ref, lk1_ref, lq2_ref,
                 lk2_ref, g_ref, o_ref, colb_ref, t_ref, first_even_ref, first_odd_ref, *, tq,
                 lambda_init):
    nq = q_ref.shape[0] // tq
    d = HEAD_DIM
    comp = lambda c: slice(c * d, (c + 1) * d)
    slope2 = LOG2E * slope_ref[:, 0:1]

    key_row = lax.broadcasted_iota(jnp.int32, (tq, LANES), 0)

    def fill(i, carry):
        rs = pl.multiple_of(i * tq, tq)
        colb_ref[pl.ds(rs, tq), :] = slope2 * (key_row + (rs + N_META)).astype(_F32)
        return carry

    lax.fori_loop(0, nq, fill, 0)
    colbm = slope2 * lax.broadcasted_iota(jnp.int32, (META_ROWS, LANES), 0).astype(_F32)
    query_col = lax.broadcasted_iota(jnp.int32, (1, tq), 1)

    meta_valid = lax.broadcasted_iota(jnp.int32, (META_ROWS, tq), 0) < N_META
    causal = (lax.broadcasted_iota(jnp.int32, (tq, tq), 1)
              >= lax.broadcasted_iota(jnp.int32, (tq, tq), 0))
    lam = (jnp.exp(jnp.sum(lq1_ref[...] * lk1_ref[...], axis=1, keepdims=True))
           - jnp.exp(jnp.sum(lq2_ref[...] * lk2_ref[...], axis=1, keepdims=True)) + lambda_init)

    def make_tile(qi, slot, odd_block, stage_slot):
        qs = qi * tq
        q = [q_ref[pl.ds(qs, tq), comp(c)] for c in range(2)]
        rq = -slope2 * (query_col + (qs + N_META)).astype(_F32)

        def scores(c, j):
            ks = _block_start(j, tq)
            return (_dot_nt(k_ref[pl.ds(ks, tq), comp(c)], q[c])
                    + _lane_tile(colb_ref[pl.ds(ks, tq), :], tq))

        def values(c, j):
            return v_ref[pl.ds(_block_start(j, tq), tq), :]

        def first_scores(c):
            tm = _dot_nt(km_ref[:, comp(c)], q[c]) + _lane_tile(colbm, tq)
            ts = [jnp.where(causal, scores(c, qi), _NEG_INF), jnp.where(meta_valid, tm, _NEG_INF)]
            if odd_block is not None:
                ts.append(scores(c, odd_block))
            return jnp.concatenate(ts, axis=0)

        def first_values(c):
            vs = [values(c, qi), vm_ref[...]]
            if odd_block is not None:
                vs.append(values(c, odd_block))
            return jnp.concatenate(vs, axis=0)

        first_ref = (first_even_ref if odd_block is None else first_odd_ref).at[stage_slot]
        tile = _QueryTile(2, 2 * d, tq, (rq, rq), first_scores, first_values, scores, values,
                          first_ref, t_ref.at[slot])
        tile.qs = qs
        return tile

    def finish(tile, state):
        _, l1, a1, _, l2, a2 = state
        o = (a1 / l1 - lam * (a2 / l2)).T
        y = _rms(o, g_ref[...]) * (1.0 - lambda_init)
        o_ref[pl.ds(tile.qs, tq), :] = y.astype(o_ref.dtype)

    _sweep_tile_pairs(nq, make_tile, finish)


def _diff_attention(proj, proj_meta, slopes, lq1, lk1, lq2, lk2, subln_g, lambda_init, nb, seq,
                    tq=ATT_TILE):
    m = proj.shape[0]
    w = 2 * HEAD_DIM
    qb, kb, vb = (i * DIFF_HEADS for i in range(3))
    tok = lambda off: pl.BlockSpec((seq, w), lambda b, h: (b, off + h))
    meta = lambda off: pl.BlockSpec((META_ROWS, w), lambda b, h: (0, off + h))
    vec = lambda n: pl.BlockSpec((1, n), lambda b, h: (0, 0))
    return pl.pallas_call(
        functools.partial(_diff_kernel, tq=tq, lambda_init=lambda_init),
        grid=(nb, DIFF_HEADS),
        in_specs=[tok(qb), tok(kb), tok(vb), meta(kb), meta(vb),
                  pl.BlockSpec((None, 1, LANES), lambda b, h: (h, 0, 0)),
                  vec(HEAD_DIM), vec(HEAD_DIM), vec(HEAD_DIM), vec(HEAD_DIM), vec(w)],
        out_specs=pl.BlockSpec((seq, w), lambda b, h: (b, h)),
        out_shape=jax.ShapeDtypeStruct((m, DIFF_HEADS * w), _BF16),
        scratch_shapes=[pltpu.VMEM((seq, LANES), _F32), pltpu.VMEM((2, 2, 2, tq, tq), _F32),
                        pltpu.VMEM((2, 2, tq + META_ROWS, tq), _F32),
                        pltpu.VMEM((2, 2, 2 * tq + META_ROWS, tq), _F32)],
        compiler_params=_params("parallel", "arbitrary"),
        name="diff_attention",
    )(proj, proj, proj, proj_meta, proj_meta, slopes, lq1, lk1, lq2, lk2, subln_g)


def _ffn(xn, w_gate, w_up, w_down, tm_down):
    hid = _gateup(xn, w_gate, w_up)
    return _mm(hid, w_down, SUBLAYER_OUT_DTYPE, tm=tm_down, tn=512)


def kernel(x, meta_tokens, ff1_pre_g, ff1_w_gate, ff1_w_up, ff1_w_down, ff1_post_g, mix_pre_g, w_in, b_forget, lambda_q1, lambda_k1, lambda_q2, lambda_k2, diff_subln_g, w_o_fox, w_o_diff, w_out, mix_post_g, ff2_pre_g, ff2_w_gate, ff2_w_up, ff2_w_down, ff2_post_g):
    nb, seq, d_model = x.shape
    depth = w_in.shape[0]
    assert depth == 1, "deeper stacks need the meta-token query path"
    h = x.reshape(nb * seq, d_model)
    hm = jnp.pad(meta_tokens.astype(x.dtype), ((0, META_ROWS - N_META), (0, 0)))
    slopes = jnp.exp2(-ALIBI_MAX_BIAS * jnp.arange(1, DIFF_HEADS + 1, dtype=_F32) / DIFF_HEADS)
    slopes = jnp.broadcast_to(slopes[:, None, None], (DIFF_HEADS, 1, LANES))
    nq = seq // ATT_TILE
    fox_w = FOX_HEADS * HEAD_DIM
    diff_w = DIFF_HEADS * 2 * HEAD_DIM
    f_col = 3 * fox_w
    vec = lambda a: a.reshape(1, -1)
    bf = lambda a: a.astype(_BF16)

    for layer in range(depth):
        lambda_init = 0.8 - 0.6 * math.exp(-0.3 * layer)

        wg, wu, wd = ff1_w_gate[layer], ff1_w_up[layer], bf(ff1_w_down[layer])
        pre, post, mix_pre = vec(ff1_pre_g[layer]), vec(ff1_post_g[layer]), vec(mix_pre_g[layer])
        y = _ffn(_norm(h, pre), wg, wu, wd, tm_down=512)
        h, u = _resid_norm(h, y, post, mix_pre, 0.5)
        ym = _ffn(_norm(hm, pre), wg, wu, wd, tm_down=META_ROWS)
        hm, um = _resid_norm(hm, ym, post, mix_pre, 0.5)

        w = w_in[layer].T
        d_col = f_col + FOX_HEADS
        proj_f = _proj(u, w, 0, 3 * fox_w, fox_w)
        proj_d = _proj(u, w, d_col, 3 * diff_w, diff_w)
        gates = _proj(u, w, d_col + 3 * diff_w, 2 * d_model, None)
        proj_fm = _proj(um, w, 0, 3 * fox_w, fox_w)
        proj_dm = _proj(um, w, d_col, 3 * diff_w, diff_w)

        b_f = jnp.pad(b_forget[layer].astype(_F32), (0, LANES - FOX_HEADS)).reshape(1, LANES)
        cm, _ = _forget_cumsum(um, w, f_col, b_f, jnp.zeros((1, LANES), _F32), 1, N_META, META_ROWS)
        c, ct = _forget_cumsum(u, w, f_col, b_f, cm[META_ROWS - 1:], nb, CUMSUM_TILE, CUMSUM_TILE)
        ct = ct.reshape(FOX_HEADS, nb * nq, 1, ATT_TILE)

        fox = _fox_attention(proj_f, proj_fm, c, cm, ct, nb, seq)
        diff = _diff_attention(proj_d, proj_dm, slopes, vec(lambda_q1[layer]).astype(_F32),
                               vec(lambda_k1[layer]).astype(_F32), vec(lambda_q2[layer]).astype(_F32),
                               vec(lambda_k2[layer]).astype(_F32), vec(diff_subln_g[layer]).astype(_F32),
                               lambda_init, nb, seq)
        merged = _merge(fox, diff, w_o_fox[layer], w_o_diff[layer], gates, 0, d_model)
        mix = _mm(merged, w_out[layer], SUBLAYER_OUT_DTYPE, tm=1024, tn=512)

        wg, wu, wd = ff2_w_gate[layer], ff2_w_up[layer], bf(ff2_w_down[layer])
        pre2, post2 = vec(ff2_pre_g[layer]), vec(ff2_post_g[layer])
        h, xn = _resid_norm(h, mix, vec(mix_post_g[layer]), pre2, 1.0)
        y = _ffn(xn, wg, wu, wd, tm_down=512)
        h = _resid(h, y, post2, 0.5)

    return h.reshape(nb, seq, d_model)
```

```python
import functools
import math

import jax
import jax.numpy as jnp
from jax import lax
from jax.experimental import pallas as pl
from jax.experimental.pallas import tpu as pltpu

N_META = 16
FOX_HEADS = 16
DIFF_HEADS = 8
HEAD_DIM = 128
RMS_EPS = 1e-6
ALIBI_MAX_BIAS = 8.0

LANES = 128
F32_SUBLANES = 8
BF16_SUBLANES = 16
META_ROWS = 128
VMEM_LIMIT = 58 * 1024 * 1024
ATT_TILE = 512
FOX_HEADS_PER_STEP = 2
LOG2E = math.log2(math.e)
CUMSUM_TILE = 256
NORM_ROWS = 512

_F32 = jnp.float32
_BF16 = jnp.bfloat16
_NEG_INF = float("-inf")
SUBLAYER_OUT_DTYPE = _BF16


def _params(*sem):
    return pltpu.CompilerParams(dimension_semantics=sem, vmem_limit_bytes=VMEM_LIMIT)


def _row_tile(m, want):
    return want if m % want == 0 else m


def _rms(x, g):
    return x * lax.rsqrt(jnp.mean(x * x, axis=-1, keepdims=True) + RMS_EPS) * g


def _sigmoid(x):
    return 0.5 * jnp.tanh(0.5 * x) + 0.5


def _dot(a, b):
    return jnp.dot(a, b, preferred_element_type=_F32)


def _dot_nt(a, b):
    return lax.dot_general(a, b, (((1,), (1,)), ((), ())), preferred_element_type=_F32)


def _norm_kernel(h_ref, g_ref, o_ref):
    o_ref[...] = _rms(h_ref[...], g_ref[...]).astype(o_ref.dtype)


def _norm(h, g, tm=NORM_ROWS):
    m, d = h.shape
    tm = _row_tile(m, tm)
    return pl.pallas_call(
        _norm_kernel,
        grid=(m // tm,),
        in_specs=[pl.BlockSpec((tm, d), lambda i: (i, 0)),
                  pl.BlockSpec((1, d), lambda i: (0, 0))],
        out_specs=pl.BlockSpec((tm, d), lambda i: (i, 0)),
        out_shape=jax.ShapeDtypeStruct((m, d), _BF16),
        compiler_params=_params("parallel"),
        name="norm",
    )(h, g)


def _resid_norm_kernel(h_ref, y_ref, gpost_ref, gpre_ref, hn_ref, xn_ref, *, alpha):
    hn = h_ref[...] + alpha * _rms(y_ref[...].astype(_F32), gpost_ref[...])
    hn_ref[...] = hn
    xn_ref[...] = _rms(hn, gpre_ref[...]).astype(xn_ref.dtype)


def _resid_norm(h, y, gpost, gpre, alpha, tm=NORM_ROWS):
    m, d = h.shape
    tm = _row_tile(m, tm)
    row = pl.BlockSpec((tm, d), lambda i: (i, 0))
    vec = pl.BlockSpec((1, d), lambda i: (0, 0))
    return pl.pallas_call(
        functools.partial(_resid_norm_kernel, alpha=alpha),
        grid=(m // tm,),
        in_specs=[row, row, vec, vec],
        out_specs=[row, row],
        out_shape=[jax.ShapeDtypeStruct((m, d), _F32), jax.ShapeDtypeStruct((m, d), _BF16)],
        compiler_params=_params("parallel"),
        name="resid_norm",
    )(h, y, gpost, gpre)


def _resid_kernel(h_ref, y_ref, gpost_ref, o_ref, *, alpha):
    o_ref[...] = h_ref[...] + alpha * _rms(y_ref[...].astype(_F32), gpost_ref[...])


def _resid(h, y, gpost, alpha, tm=NORM_ROWS):
    m, d = h.shape
    tm = _row_tile(m, tm)
    row = pl.BlockSpec((tm, d), lambda i: (i, 0))
    vec = pl.BlockSpec((1, d), lambda i: (0, 0))
    return pl.pallas_call(
        functools.partial(_resid_kernel, alpha=alpha),
        grid=(m // tm,),
        in_specs=[row, row, vec],
        out_specs=row,
        out_shape=jax.ShapeDtypeStruct((m, d), _F32),
        compiler_params=_params("parallel"),
        name="resid",
    )(h, y, gpost)


def _matmul_call(body, n_in, grid, in_specs, out_spec, out_shape, args, casts, name):
    steps = grid[0] * grid[1]
    cast_specs, cast_shapes = [], []
    for w in casts:
        rows, cols = w.shape
        slab = -(-rows // steps)
        slab += -slab % BF16_SUBLANES
        last = -(-rows // slab) - 1
        cast_specs.append(pl.BlockSpec(
            (slab, cols), lambda i, j, last=last: (jnp.minimum(i * grid[1] + j, last), 0)))
        cast_shapes.append(jax.ShapeDtypeStruct(w.shape, _BF16))

    def kernel(*refs):
        n_cast = len(casts)
        ins, src = refs[:n_in], refs[n_in:n_in + n_cast]
        o_ref, dst = refs[n_in + n_cast], refs[n_in + n_cast + 1:]
        for s, d in zip(src, dst):
            d[...] = s[...].astype(d.dtype)
        body(*ins, o_ref)

    out = pl.pallas_call(
        kernel,
        grid=grid,
        in_specs=[*in_specs, *cast_specs],
        out_specs=[out_spec, *cast_specs],
        out_shape=[out_shape, *cast_shapes],
        compiler_params=_params("arbitrary", "arbitrary"),
        name=name,
    )(*args, *casts)
    return out[0], list(out[1:])


def _mm_kernel(a_ref, w_ref, o_ref):
    o_ref[...] = _dot(a_ref[...], w_ref[...].astype(a_ref.dtype)).astype(o_ref.dtype)


def _mm(a, w, out_dtype, tm, tn, casts=()):
    m, k = a.shape
    n = w.shape[1]
    tm = _row_tile(m, tm)
    return _matmul_call(
        _mm_kernel, 2, (m // tm, n // tn),
        [pl.BlockSpec((tm, k), lambda i, j: (i, 0)), pl.BlockSpec((k, tn), lambda i, j: (0, j))],
        pl.BlockSpec((tm, tn), lambda i, j: (i, j)), jax.ShapeDtypeStruct((m, n), out_dtype),
        [a, w], list(casts), "mm")


def _gateup_kernel(x_ref, wg_ref, wu_ref, o_ref):
    x = x_ref[...]
    g = _dot(x, wg_ref[...].astype(x.dtype))
    u = _dot(x, wu_ref[...].astype(x.dtype))
    o_ref[...] = (g * _sigmoid(g) * u).astype(o_ref.dtype)


def _gateup(x, wg, wu, tm=2048, tn=256, casts=()):
    m, k = x.shape
    n = wg.shape[1]
    tm = _row_tile(m, tm)
    wspec = pl.BlockSpec((k, tn), lambda i, j: (0, j))
    return _matmul_call(
        _gateup_kernel, 3, (m // tm, n // tn),
        [pl.BlockSpec((tm, k), lambda i, j: (i, 0), pipeline_mode=pl.Buffered(1)), wspec, wspec],
        pl.BlockSpec((tm, tn), lambda i, j: (i, j)), jax.ShapeDtypeStruct((m, n), _BF16),
        [x, wg, wu], list(casts), "gateup")


def _proj_kernel(u_ref, wt_ref, o_ref, *, tn, q_cols, q_scale):
    acc = _dot_nt(u_ref[...], wt_ref[...].astype(u_ref.dtype))
    if q_cols is None:
        o_ref[...] = _sigmoid(acc).astype(o_ref.dtype)
    else:
        scale = jnp.where(pl.program_id(1) * tn < q_cols, q_scale, 1.0).astype(_F32)
        o_ref[...] = (acc * scale).astype(o_ref.dtype)


def _proj(u, w_in_t, col0, n_out, q_cols, tm=1024, tn=512):
    m, k = u.shape
    tm = _row_tile(m, tm)
    row_align = BF16_SUBLANES if w_in_t.dtype == _BF16 else F32_SUBLANES
    assert col0 % row_align == 0
    return pl.pallas_call(
        functools.partial(_proj_kernel, tn=tn, q_cols=q_cols, q_scale=HEAD_DIM ** -0.5 * LOG2E),
        grid=(m // tm, n_out // tn),
        in_specs=[pl.BlockSpec((tm, k), lambda i, j: (i, 0)),
                  pl.BlockSpec((pl.Element(tn), pl.Element(k)),
                               lambda i, j: (pl.multiple_of(col0 + j * tn, row_align), 0))],
        out_specs=pl.BlockSpec((tm, tn), lambda i, j: (i, j)),
        out_shape=jax.ShapeDtypeStruct((m, n_out), _BF16),
        compiler_params=_params("parallel", "arbitrary"),
        name="proj_gate" if q_cols is None else "proj_qkv",
    )(u, w_in_t)


def _merge_kernel(af_ref, ad_ref, wf_ref, wd_ref, gf_ref, gd_ref, o_ref):
    yf = _dot(af_ref[...], wf_ref[...].astype(af_ref.dtype))
    yd = _dot(ad_ref[...], wd_ref[...].astype(ad_ref.dtype))
    o_ref[...] = (gf_ref[...].astype(_F32) * yf + gd_ref[...].astype(_F32) * yd).astype(o_ref.dtype)


def _merge(fox, diff, w_of, w_od, gates, gf_col, gd_col, tm=1024, tn=512):
    m, k = fox.shape
    n = w_of.shape[1]
    aspec = pl.BlockSpec((tm, k), lambda i, j: (i, 0))
    wspec = pl.BlockSpec((k, tn), lambda i, j: (0, j))
    return pl.pallas_call(
        _merge_kernel,
        grid=(m // tm, n // tn),
        in_specs=[aspec, aspec, wspec, wspec,
                  pl.BlockSpec((tm, tn), lambda i, j: (i, gf_col // tn + j)),
                  pl.BlockSpec((tm, tn), lambda i, j: (i, gd_col // tn + j))],
        out_specs=pl.BlockSpec((tm, tn), lambda i, j: (i, j)),
        out_shape=jax.ShapeDtypeStruct((m, n), _BF16),
        compiler_params=_params("parallel", "arbitrary"),
        name="merge",
    )(fox, diff, w_of, w_od, gates, gates)


def _cumsum_kernel(u_ref, wf_ref, bf_ref, init_ref, c_ref, ct_ref, carry_ref, *, n_valid):
    tm = u_ref.shape[0]

    @pl.when(pl.program_id(1) == 0)
    def _():
        carry_ref[...] = init_ref[...]

    is_head = lax.broadcasted_iota(jnp.int32, wf_ref.shape, 0) < FOX_HEADS
    wf = jnp.where(is_head, wf_ref[...], 0.0).astype(u_ref.dtype)
    x = _dot_nt(u_ref[...], wf) + bf_ref[...]
    logf = jnp.minimum(x, 0.0) - jnp.log1p(jnp.exp(-jnp.abs(x)))
    row = lax.broadcasted_iota(jnp.int32, (tm, tm), 0)
    col = lax.broadcasted_iota(jnp.int32, (tm, tm), 1)
    if n_valid < tm:
        logf = jnp.where(lax.broadcasted_iota(jnp.int32, (tm, LANES), 0) < n_valid, logf, 0.0)
    tri = (row >= col).astype(_F32)
    c = jnp.dot(tri, logf, precision=lax.Precision.HIGHEST, preferred_element_type=_F32)
    c = c + carry_ref[...]
    c_ref[...] = c
    ct_ref[...] = c.T[:FOX_HEADS, :]
    carry_ref[...] = c[tm - 1:tm, :]


def _forget_cumsum(u, w_in_t, f_col, bf, init, nb, n_valid, tm):
    m, k = u.shape
    nt = m // (nb * tm)
    f_blk = f_col // LANES
    assert f_blk * LANES == f_col
    return pl.pallas_call(
        functools.partial(_cumsum_kernel, n_valid=n_valid),
        grid=(nb, nt),
        in_specs=[pl.BlockSpec((tm, k), lambda b, t: (b * nt + t, 0)),
                  pl.BlockSpec((LANES, k), lambda b, t: (f_blk, 0)),
                  pl.BlockSpec((1, LANES), lambda b, t: (0, 0)),
                  pl.BlockSpec((1, LANES), lambda b, t: (0, 0))],
        out_specs=[pl.BlockSpec((tm, LANES), lambda b, t: (b * nt + t, 0)),
                   pl.BlockSpec((FOX_HEADS, tm), lambda b, t: (0, b * nt + t))],
        out_shape=[jax.ShapeDtypeStruct((m, LANES), _F32),
                   jax.ShapeDtypeStruct((FOX_HEADS, m), _F32)],
        scratch_shapes=[pltpu.VMEM((1, LANES), _F32)],
        compiler_params=_params("arbitrary", "arbitrary"),
        name="forget_cumsum",
    )(u, w_in_t, bf, init)


def _dot_tn(a, b):
    return lax.dot_general(a, b, (((0,), (0,)), ((), ())), preferred_element_type=_F32)


def _lane_tile(x, n):
    return jnp.concatenate([x] * (n // LANES), axis=1)


def _flash_step(t, v, rq, m, l, acc):
    m_new = jnp.maximum(m, jnp.max(t, axis=0, keepdims=True) + rq)
    alpha = jnp.exp2(m - m_new)
    p = jnp.exp2(t - (m_new - rq))
    l = alpha * l + jnp.sum(p, axis=0, keepdims=True)
    acc = alpha * acc + _dot_tn(v, p.astype(v.dtype))
    return m_new, l, acc


def _flash_init(dv, tq):
    return (jnp.full((1, tq), _NEG_INF, _F32), jnp.zeros((1, tq), _F32), jnp.zeros((dv, tq), _F32))


def _block_start(j, size):
    return j * size if isinstance(j, int) else pl.multiple_of(j * size, size)


class _QueryTile:
    def __init__(self, streams, dv, tq, rq, first_scores, first_values, scores, values,
                 first_ref, t_ref):
        self.streams, self.dv, self.tq, self.rq = streams, dv, tq, rq
        self.first_scores, self.first_values = first_scores, first_values
        self.scores, self.values, self.first_ref, self.t_ref = scores, values, first_ref, t_ref

    def stage_first(self):
        for s in range(self.streams):
            t = self.first_scores(s)
            self.first_ref[s, :t.shape[0], :] = t
        self.first_keys = t.shape[0]

    def start(self, n_pairs):
        if n_pairs:
            for s in range(self.streams):
                self.t_ref[0, s] = self.scores(s, 0)
        state = []
        for s in range(self.streams):
            state.extend(_flash_step(self.first_ref[s, :self.first_keys, :], self.first_values(s),
                                     self.rq[s], *_flash_init(self.dv, self.tq)))
        return tuple(state)

    def run_pairs(self, n_pairs, state):
        def half(st, slot, blk, nxt):
            for s in range(self.streams):
                self.t_ref[1 - slot, s] = self.scores(s, nxt)
            out = []
            for s in range(self.streams):
                out.extend(_flash_step(self.t_ref[slot, s], self.values(s, blk), self.rq[s],
                                       *st[3 * s:3 * s + 3]))
            return tuple(out)

        def pair(j, st):
            st = half(st, 0, 2 * j, 2 * j + 1)
            return half(st, 1, 2 * j + 1, jnp.minimum(2 * j + 2, 2 * n_pairs - 1))

        return lax.fori_loop(0, n_pairs, pair, state) if n_pairs else state


def _sweep_tile_pairs(n_tiles, make_tile, finish):
    def pair_tiles(a):
        tiles = [make_tile(2 * a, 0, None, a % 2), make_tile(2 * a + 1, 1, 2 * a, a % 2)]
        for t in tiles:
            t.stage_first()
        return tiles

    tiles = pair_tiles(0)
    for a in range(n_tiles // 2):
        nxt = pair_tiles(a + 1) if a + 1 < n_tiles // 2 else None
        states = [t.start(a) for t in tiles]
        states = [t.run_pairs(a, st) for t, st in zip(tiles, states)]
        for t, st in zip(tiles, states):
            finish(t, st)
        tiles = nxt


def _fox_kernel(q_ref, k_ref, v_ref, km_ref, vm_ref, c_ref, cm_ref, ct_ref, o_ref, colb_ref,
                colbm_ref, t_ref, first_even_ref, first_odd_ref, *, tq, hps):
    nq = q_ref.shape[0] // tq
    d = HEAD_DIM
    heads = [pl.program_id(1) * hps + hh for hh in range(hps)]
    lanes = lambda hh: slice(hh * d, (hh + 1) * d)

    lane = lax.broadcasted_iota(jnp.int32, (tq, LANES), 1)
    lane_m = lax.broadcasted_iota(jnp.int32, (META_ROWS, LANES), 1)
    for hh in range(hps):
        def fill(i, carry, hh=hh):
            rs = pl.multiple_of(i * tq, tq)
            ck = jnp.sum(jnp.where(lane == heads[hh], c_ref[pl.ds(rs, tq), :], 0.0),
                         axis=1, keepdims=True)
            colb_ref[hh, pl.ds(rs, tq), :] = jnp.broadcast_to(-LOG2E * ck, (tq, LANES))
            return carry

        lax.fori_loop(0, nq, fill, 0)
        ckm = jnp.sum(jnp.where(lane_m == heads[hh], cm_ref[...], 0.0), axis=1, keepdims=True)
        colbm_ref[hh] = jnp.broadcast_to(-LOG2E * ckm, (META_ROWS, LANES))

    meta_valid = lax.broadcasted_iota(jnp.int32, (META_ROWS, tq), 0) < N_META
    causal = (lax.broadcasted_iota(jnp.int32, (tq, tq), 1)
              >= lax.broadcasted_iota(jnp.int32, (tq, tq), 0))

    def make_tile(qi, slot, odd_block, stage_slot):
        qs = qi * tq
        q = [q_ref[pl.ds(qs, tq), lanes(hh)] for hh in range(hps)]
        rq = [LOG2E * ct_ref[hh, qi] for hh in range(hps)]

        def scores(hh, j):
            ks = _block_start(j, tq)
            return (_dot_nt(k_ref[pl.ds(ks, tq), lanes(hh)], q[hh])
                    + _lane_tile(colb_ref[hh, pl.ds(ks, tq), :], tq))

        def values(hh, j):
            return v_ref[pl.ds(_block_start(j, tq), tq), lanes(hh)]

        def first_scores(hh):
            tm = _dot_nt(km_ref[:, lanes(hh)], q[hh]) + _lane_tile(colbm_ref[hh], tq)
            ts = [jnp.where(causal, scores(hh, qi), _NEG_INF), jnp.where(meta_valid, tm, _NEG_INF)]
            if odd_block is not None:
                ts.append(scores(hh, odd_block))
            return jnp.concatenate(ts, axis=0)

        def first_values(hh):
            vs = [values(hh, qi), vm_ref[:, lanes(hh)]]
            if odd_block is not None:
                vs.append(values(hh, odd_block))
            return jnp.concatenate(vs, axis=0)

        first_ref = (first_even_ref if odd_block is None else first_odd_ref).at[stage_slot]
        tile = _QueryTile(hps, d, tq, rq, first_scores, first_values, scores, values, first_ref,
                          t_ref.at[slot])
        tile.qs = qs
        return tile

    def finish(tile, state):
        for hh in range(hps):
            _, l, acc = state[3 * hh:3 * hh + 3]
            o_ref[pl.ds(tile.qs, tq), lanes(hh)] = (acc / l).T.astype(o_ref.dtype)

    _sweep_tile_pairs(nq, make_tile, finish)


def _fox_attention(proj, proj_meta, c, cm, ct, nb, seq, tq=ATT_TILE, hps=FOX_HEADS_PER_STEP):
    m = proj.shape[0]
    w = hps * HEAD_DIM
    qb, kb, vb = (i * FOX_HEADS // hps for i in range(3))
    nq = seq // tq
    tok = lambda off: pl.BlockSpec((seq, w), lambda b, s: (b, off + s))
    meta = lambda off: pl.BlockSpec((META_ROWS, w), lambda b, s: (0, off + s))
    return pl.pallas_call(
        functools.partial(_fox_kernel, tq=tq, hps=hps),
        grid=(nb, FOX_HEADS // hps),
        in_specs=[tok(qb), tok(kb), tok(vb), meta(kb), meta(vb),
                  pl.BlockSpec((seq, LANES), lambda b, s: (b, 0), pipeline_mode=pl.Buffered(1)),
                  pl.BlockSpec((META_ROWS, LANES), lambda b, s: (0, 0)),
                  pl.BlockSpec((hps, nq, 1, tq), lambda b, s: (s, b, 0, 0))],
        out_specs=pl.BlockSpec((seq, w), lambda b, s: (b, s)),
        out_shape=jax.ShapeDtypeStruct((m, FOX_HEADS * HEAD_DIM), _BF16),
        scratch_shapes=[pltpu.VMEM((hps, seq, LANES), _F32),
                        pltpu.VMEM((hps, META_ROWS, LANES), _F32),
                        pltpu.VMEM((2, 2, hps, tq, tq), _F32),
                        pltpu.VMEM((2, hps, tq + META_ROWS, tq), _F32),
                        pltpu.VMEM((2, hps, 2 * tq + META_ROWS, tq), _F32)],
        compiler_params=_params("parallel", "arbitrary"),
        name="fox_attention",
    )(proj, proj, proj, proj_meta, proj_meta, c, cm, ct)


def _diff_kernel(q_ref, k_ref, v_ref, km_ref, vm_ref, slope_ref, lq1_ref, lk1_ref, lq2_ref,
                 lk2_ref, g_ref, o_ref, colb_ref, t_ref, first_even_ref, first_odd_ref, *, tq,
                 lambda_init):
    nq = q_ref.shape[0] // tq
    d = HEAD_DIM
    comp = lambda c: slice(c * d, (c + 1) * d)
    slope2 = LOG2E * slope_ref[:, 0:1]

    key_row = lax.broadcasted_iota(jnp.int32, (tq, LANES), 0)

    def fill(i, carry):
        rs = pl.multiple_of(i * tq, tq)
        colb_ref[pl.ds(rs, tq), :] = slope2 * (key_row + (rs + N_META)).astype(_F32)
        return carry

    lax.fori_loop(0, nq, fill, 0)
    colbm = slope2 * lax.broadcasted_iota(jnp.int32, (META_ROWS, LANES), 0).astype(_F32)
    query_col = lax.broadcasted_iota(jnp.int32, (1, tq), 1)

    meta_valid = lax.broadcasted_iota(jnp.int32, (META_ROWS, tq), 0) < N_META
    causal = (lax.broadcasted_iota(jnp.int32, (tq, tq), 1)
              >= lax.broadcasted_iota(jnp.int32, (tq, tq), 0))
    lam = (jnp.exp(jnp.sum(lq1_ref[...] * lk1_ref[...], axis=1, keepdims=True))
           - jnp.exp(jnp.sum(lq2_ref[...] * lk2_ref[...], axis=1, keepdims=True)) + lambda_init)

    def make_tile(qi, slot, odd_block, stage_slot):
        qs = qi * tq
        q = [q_ref[pl.ds(qs, tq), comp(c)] for c in range(2)]
        rq = -slope2 * (query_col + (qs + N_META)).astype(_F32)

        def scores(c, j):
            ks = _block_start(j, tq)
            return (_dot_nt(k_ref[pl.ds(ks, tq), comp(c)], q[c])
                    + _lane_tile(colb_ref[pl.ds(ks, tq), :], tq))

        def values(c, j):
            return v_ref[pl.ds(_block_start(j, tq), tq), :]

        def first_scores(c):
            tm = _dot_nt(km_ref[:, comp(c)], q[c]) + _lane_tile(colbm, tq)
            ts = [jnp.where(causal, scores(c, qi), _NEG_INF), jnp.where(meta_valid, tm, _NEG_INF)]
            if odd_block is not None:
                ts.append(scores(c, odd_block))
            return jnp.concatenate(ts, axis=0)

        def first_values(c):
            vs = [values(c, qi), vm_ref[...]]
            if odd_block is not None:
                vs.append(values(c, odd_block))
            return jnp.concatenate(vs, axis=0)

        first_ref = (first_even_ref if odd_block is None else first_odd_ref).at[stage_slot]
        tile = _QueryTile(2, 2 * d, tq, (rq, rq), first_scores, first_values, scores, values,
                          first_ref, t_ref.at[slot])
        tile.qs = qs
        return tile

    def finish(tile, state):
        _, l1, a1, _, l2, a2 = state
        o = (a1 / l1 - lam * (a2 / l2)).T
        y = _rms(o, g_ref[...]) * (1.0 - lambda_init)
        o_ref[pl.ds(tile.qs, tq), :] = y.astype(o_ref.dtype)

    _sweep_tile_pairs(nq, make_tile, finish)


def _diff_attention(proj, proj_meta, slopes, lq1, lk1, lq2, lk2, subln_g, lambda_init, nb, seq,
                    tq=ATT_TILE):
    m = proj.shape[0]
    w = 2 * HEAD_DIM
    qb, kb, vb = (i * DIFF_HEADS for i in range(3))
    tok = lambda off: pl.BlockSpec((seq, w), lambda b, h: (b, off + h))
    meta = lambda off: pl.BlockSpec((META_ROWS, w), lambda b, h: (0, off + h))
    vec = lambda n: pl.BlockSpec((1, n), lambda b, h: (0, 0))
    return pl.pallas_call(
        functools.partial(_diff_kernel, tq=tq, lambda_init=lambda_init),
        grid=(nb, DIFF_HEADS),
        in_specs=[tok(qb), tok(kb), tok(vb), meta(kb), meta(vb),
                  pl.BlockSpec((None, 1, LANES), lambda b, h: (h, 0, 0)),
                  vec(HEAD_DIM), vec(HEAD_DIM), vec(HEAD_DIM), vec(HEAD_DIM), vec(w)],
        out_specs=pl.BlockSpec((seq, w), lambda b, h: (b, h)),
        out_shape=jax.ShapeDtypeStruct((m, DIFF_HEADS * w), _BF16),
        scratch_shapes=[pltpu.VMEM((seq, LANES), _F32), pltpu.VMEM((2, 2, 2, tq, tq), _F32),
                        pltpu.VMEM((2, 2, tq + META_ROWS, tq), _F32),
                        pltpu.VMEM((2, 2, 2 * tq + META_ROWS, tq), _F32)],
        compiler_params=_params("parallel", "arbitrary"),
        name="diff_attention",
    )(proj, proj, proj, proj_meta, proj_meta, slopes, lq1, lk1, lq2, lk2, subln_g)


def kernel(x, meta_tokens, ff1_pre_g, ff1_w_gate, ff1_w_up, ff1_w_down, ff1_post_g, mix_pre_g, w_in, b_forget, lambda_q1, lambda_k1, lambda_q2, lambda_k2, diff_subln_g, w_o_fox, w_o_diff, w_out, mix_post_g, ff2_pre_g, ff2_w_gate, ff2_w_up, ff2_w_down, ff2_post_g):
    nb, seq, d_model = x.shape
    depth = w_in.shape[0]
    assert depth == 1, "deeper stacks need the meta-token query path"
    h = x.reshape(nb * seq, d_model)
    hm = jnp.pad(meta_tokens.astype(x.dtype), ((0, META_ROWS - N_META), (0, 0)))
    slopes = jnp.exp2(-ALIBI_MAX_BIAS * jnp.arange(1, DIFF_HEADS + 1, dtype=_F32) / DIFF_HEADS)
    slopes = jnp.broadcast_to(slopes[:, None, None], (DIFF_HEADS, 1, LANES))
    nq = seq // ATT_TILE
    fox_w = FOX_HEADS * HEAD_DIM
    diff_w = DIFF_HEADS * 2 * HEAD_DIM
    f_col = 3 * fox_w
    vec = lambda a: a.reshape(1, -1)

    for layer in range(depth):
        lambda_init = 0.8 - 0.6 * math.exp(-0.3 * layer)

        wg, wu = ff1_w_gate[layer], ff1_w_up[layer]
        pre, post, mix_pre = vec(ff1_pre_g[layer]), vec(ff1_post_g[layer]), vec(mix_pre_g[layer])
        hid, (wd, w) = _gateup(_norm(h, pre), wg, wu, casts=[ff1_w_down[layer], w_in[layer].T])
        y, (w_of, w_od, w_mix) = _mm(hid, wd, SUBLAYER_OUT_DTYPE, tm=512, tn=512,
                                     casts=[w_o_fox[layer], w_o_diff[layer], w_out[layer]])
        h, u = _resid_norm(h, y, post, mix_pre, 0.5)
        hid_m, _ = _gateup(_norm(hm, pre), wg, wu)
        ym, _ = _mm(hid_m, wd, SUBLAYER_OUT_DTYPE, tm=META_ROWS, tn=512)
        hm, um = _resid_norm(hm, ym, post, mix_pre, 0.5)

        d_col = f_col + FOX_HEADS
        proj_f = _proj(u, w, 0, 3 * fox_w, fox_w)
        proj_d = _proj(u, w, d_col, 3 * diff_w, diff_w)
        gates = _proj(u, w, d_col + 3 * diff_w, 2 * d_model, None)
        proj_fm = _proj(um, w, 0, 3 * fox_w, fox_w)
        proj_dm = _proj(um, w, d_col, 3 * diff_w, diff_w)

        b_f = jnp.pad(b_forget[layer].astype(_F32), (0, LANES - FOX_HEADS)).reshape(1, LANES)
        cm, _ = _forget_cumsum(um, w, f_col, b_f, jnp.zeros((1, LANES), _F32), 1, N_META, META_ROWS)
        c, ct = _forget_cumsum(u, w, f_col, b_f, cm[META_ROWS - 1:], nb, CUMSUM_TILE, CUMSUM_TILE)
        ct = ct.reshape(FOX_HEADS, nb * nq, 1, ATT_TILE)

        fox = _fox_attention(proj_f, proj_fm, c, cm, ct, nb, seq)
        diff = _diff_attention(proj_d, proj_dm, slopes, vec(lambda_q1[layer]).astype(_F32),
                               vec(lambda_k1[layer]).astype(_F32), vec(lambda_q2[layer]).astype(_F32),
                               vec(lambda_k2[layer]).astype(_F32), vec(diff_subln_g[layer]).astype(_F32),
                               lambda_init, nb, seq)
        merged = _merge(fox, diff, w_of, w_od, gates, 0, d_model)
        mix, _ = _mm(merged, w_mix, SUBLAYER_OUT_DTYPE, tm=1024, tn=512)

        pre2, post2 = vec(ff2_pre_g[layer]), vec(ff2_post_g[layer])
        h, xn = _resid_norm(h, mix, vec(mix_post_g[layer]), pre2, 1.0)
        hid, (wd2,) = _gateup(xn, ff2_w_gate[layer], ff2_w_up[layer], casts=[ff2_w_down[layer]])
        y, _ = _mm(hid, wd2, SUBLAYER_OUT_DTYPE, tm=512, tn=512)
        h = _resid(h, y, post2, 0.5)

    return h.reshape(nb, seq, d_model)
```

```python
import functools
import math

import jax
import jax.numpy as jnp
from jax import lax
from jax.experimental import pallas as pl
from jax.experimental.pallas import tpu as pltpu

N_META = 16
FOX_HEADS = 16
DIFF_HEADS = 8
HEAD_DIM = 128
RMS_EPS = 1e-6
ALIBI_MAX_BIAS = 8.0

LANES = 128
F32_SUBLANES = 8
BF16_SUBLANES = 16
META_ROWS = 128
VMEM_LIMIT = 58 * 1024 * 1024
ATT_TILE = 512
FOX_HEADS_PER_STEP = 2
LOG2E = math.log2(math.e)
CUMSUM_TILE = 256
NORM_ROWS = 512

_F32 = jnp.float32
_BF16 = jnp.bfloat16
_NEG_INF = float("-inf")
SUBLAYER_OUT_DTYPE = _BF16


def _params(*sem):
    return pltpu.CompilerParams(dimension_semantics=sem, vmem_limit_bytes=VMEM_LIMIT)


def _row_tile(m, want):
    return want if m % want == 0 else m


def _rms(x, g):
    return x * lax.rsqrt(jnp.mean(x * x, axis=-1, keepdims=True) + RMS_EPS) * g


def _sigmoid(x):
    return 0.5 * jnp.tanh(0.5 * x) + 0.5


def _dot(a, b):
    return jnp.dot(a, b, preferred_element_type=_F32)


def _dot_nt(a, b):
    return lax.dot_general(a, b, (((1,), (1,)), ((), ())), preferred_element_type=_F32)


def _norm_kernel(h_ref, g_ref, o_ref):
    o_ref[...] = _rms(h_ref[...], g_ref[...]).astype(o_ref.dtype)


def _norm(h, g, tm=NORM_ROWS):
    m, d = h.shape
    tm = _row_tile(m, tm)
    return pl.pallas_call(
        _norm_kernel,
        grid=(m // tm,),
        in_specs=[pl.BlockSpec((tm, d), lambda i: (i, 0)),
                  pl.BlockSpec((1, d), lambda i: (0, 0))],
        out_specs=pl.BlockSpec((tm, d), lambda i: (i, 0)),
        out_shape=jax.ShapeDtypeStruct((m, d), _BF16),
        compiler_params=_params("parallel"),
        name="norm",
    )(h, g)


def _resid_norm_kernel(h_ref, y_ref, gpost_ref, gpre_ref, hn_ref, xn_ref, *, alpha):
    hn = h_ref[...] + alpha * _rms(y_ref[...].astype(_F32), gpost_ref[...])
    hn_ref[...] = hn
    xn_ref[...] = _rms(hn, gpre_ref[...]).astype(xn_ref.dtype)


def _resid_norm(h, y, gpost, gpre, alpha, tm=NORM_ROWS):
    m, d = h.shape
    tm = _row_tile(m, tm)
    row = pl.BlockSpec((tm, d), lambda i: (i, 0))
    vec = pl.BlockSpec((1, d), lambda i: (0, 0))
    return pl.pallas_call(
        functools.partial(_resid_norm_kernel, alpha=alpha),
        grid=(m // tm,),
        in_specs=[row, row, vec, vec],
        out_specs=[row, row],
        out_shape=[jax.ShapeDtypeStruct((m, d), _F32), jax.ShapeDtypeStruct((m, d), _BF16)],
        compiler_params=_params("parallel"),
        name="resid_norm",
    )(h, y, gpost, gpre)


def _resid_kernel(h_ref, y_ref, gpost_ref, o_ref, *, alpha):
    o_ref[...] = h_ref[...] + alpha * _rms(y_ref[...].astype(_F32), gpost_ref[...])


def _resid(h, y, gpost, alpha, tm=NORM_ROWS):
    m, d = h.shape
    tm = _row_tile(m, tm)
    row = pl.BlockSpec((tm, d), lambda i: (i, 0))
    vec = pl.BlockSpec((1, d), lambda i: (0, 0))
    return pl.pallas_call(
        functools.partial(_resid_kernel, alpha=alpha),
        grid=(m // tm,),
        in_specs=[row, row, vec],
        out_specs=row,
        out_shape=jax.ShapeDtypeStruct((m, d), _F32),
        compiler_params=_params("parallel"),
        name="resid",
    )(h, y, gpost)


def _matmul_call(body, n_in, grid, in_specs, out_spec, out_shape, args, casts, name):
    steps = grid[0] * grid[1]
    cast_specs, cast_shapes = [], []
    for w in casts:
        rows, cols = w.shape
        slab = -(-rows // steps)
        slab += -slab % BF16_SUBLANES
        last = -(-rows // slab) - 1
        cast_specs.append(pl.BlockSpec(
            (slab, cols), lambda i, j, last=last: (jnp.minimum(i * grid[1] + j, last), 0)))
        cast_shapes.append(jax.ShapeDtypeStruct(w.shape, _BF16))

    def kernel(*refs):
        n_cast = len(casts)
        ins, src = refs[:n_in], refs[n_in:n_in + n_cast]
        o_ref, dst = refs[n_in + n_cast], refs[n_in + n_cast + 1:]
        for s, d in zip(src, dst):
            d[...] = s[...].astype(d.dtype)
        body(*ins, o_ref)

    out = pl.pallas_call(
        kernel,
        grid=grid,
        in_specs=[*in_specs, *cast_specs],
        out_specs=[out_spec, *cast_specs],
        out_shape=[out_shape, *cast_shapes],
        compiler_params=_params("arbitrary", "arbitrary"),
        name=name,
    )(*args, *casts)
    return out[0], list(out[1:])


def _mm_kernel(a_ref, w_ref, o_ref):
    o_ref[...] = _dot(a_ref[...], w_ref[...].astype(a_ref.dtype)).astype(o_ref.dtype)


def _mm(a, w, out_dtype, tm, tn, casts=()):
    m, k = a.shape
    n = w.shape[1]
    tm = _row_tile(m, tm)
    return _matmul_call(
        _mm_kernel, 2, (m // tm, n // tn),
        [pl.BlockSpec((tm, k), lambda i, j: (i, 0)), pl.BlockSpec((k, tn), lambda i, j: (0, j))],
        pl.BlockSpec((tm, tn), lambda i, j: (i, j)), jax.ShapeDtypeStruct((m, n), out_dtype),
        [a, w], list(casts), "mm")


def _gateup_kernel(x_ref, wg_ref, wu_ref, o_ref):
    x = x_ref[...]
    g = _dot(x, wg_ref[...].astype(x.dtype))
    u = _dot(x, wu_ref[...].astype(x.dtype))
    o_ref[...] = (g * _sigmoid(g) * u).astype(o_ref.dtype)


def _gateup(x, wg, wu, tm=2048, tn=256, casts=()):
    m, k = x.shape
    n = wg.shape[1]
    tm = _row_tile(m, tm)
    wspec = pl.BlockSpec((k, tn), lambda i, j: (0, j))
    return _matmul_call(
        _gateup_kernel, 3, (m // tm, n // tn),
        [pl.BlockSpec((tm, k), lambda i, j: (i, 0), pipeline_mode=pl.Buffered(1)), wspec, wspec],
        pl.BlockSpec((tm, tn), lambda i, j: (i, j)), jax.ShapeDtypeStruct((m, n), _BF16),
        [x, wg, wu], list(casts), "gateup")


def _proj_kernel(u_ref, wt_ref, o_ref, *, tn, q_cols, q_scale):
    acc = _dot_nt(u_ref[...], wt_ref[...].astype(u_ref.dtype))
    if q_cols is None:
        o_ref[...] = _sigmoid(acc).astype(o_ref.dtype)
    else:
        scale = jnp.where(pl.program_id(1) * tn < q_cols, q_scale, 1.0).astype(_F32)
        o_ref[...] = (acc * scale).astype(o_ref.dtype)


def _proj(u, w_in_t, col0, n_out, q_cols, tm=1024, tn=512):
    m, k = u.shape
    tm = _row_tile(m, tm)
    row_align = BF16_SUBLANES if w_in_t.dtype == _BF16 else F32_SUBLANES
    assert col0 % row_align == 0
    return pl.pallas_call(
        functools.partial(_proj_kernel, tn=tn, q_cols=q_cols, q_scale=HEAD_DIM ** -0.5 * LOG2E),
        grid=(m // tm, n_out // tn),
        in_specs=[pl.BlockSpec((tm, k), lambda i, j: (i, 0)),
                  pl.BlockSpec((pl.Element(tn), pl.Element(k)),
                               lambda i, j: (pl.multiple_of(col0 + j * tn, row_align), 0))],
        out_specs=pl.BlockSpec((tm, tn), lambda i, j: (i, j)),
        out_shape=jax.ShapeDtypeStruct((m, n_out), _BF16),
        compiler_params=_params("parallel", "arbitrary"),
        name="proj_gate" if q_cols is None else "proj_qkv",
    )(u, w_in_t)


def _merge_kernel(af_ref, ad_ref, wf_ref, wd_ref, gf_ref, gd_ref, o_ref):
    yf = _dot(af_ref[...], wf_ref[...].astype(af_ref.dtype))
    yd = _dot(ad_ref[...], wd_ref[...].astype(ad_ref.dtype))
    o_ref[...] = (gf_ref[...].astype(_F32) * yf + gd_ref[...].astype(_F32) * yd).astype(o_ref.dtype)


def _merge(fox, diff, w_of, w_od, gates, gf_col, gd_col, tm=1024, tn=512):
    m, k = fox.shape
    n = w_of.shape[1]
    aspec = pl.BlockSpec((tm, k), lambda i, j: (i, 0))
    wspec = pl.BlockSpec((k, tn), lambda i, j: (0, j))
    return pl.pallas_call(
        _merge_kernel,
        grid=(m // tm, n // tn),
        in_specs=[aspec, aspec, wspec, wspec,
                  pl.BlockSpec((tm, tn), lambda i, j: (i, gf_col // tn + j)),
                  pl.BlockSpec((tm, tn), lambda i, j: (i, gd_col // tn + j))],
        out_specs=pl.BlockSpec((tm, tn), lambda i, j: (i, j)),
        out_shape=jax.ShapeDtypeStruct((m, n), _BF16),
        compiler_params=_params("parallel", "arbitrary"),
        name="merge",
    )(fox, diff, w_of, w_od, gates, gates)


def _cumsum_kernel(u_ref, wf_ref, bf_ref, init_ref, c_ref, ct_ref, carry_ref, *, n_valid):
    tm = u_ref.shape[0]

    @pl.when(pl.program_id(1) == 0)
    def _():
        carry_ref[...] = init_ref[...]

    is_head = lax.broadcasted_iota(jnp.int32, wf_ref.shape, 0) < FOX_HEADS
    wf = jnp.where(is_head, wf_ref[...], 0.0).astype(u_ref.dtype)
    x = _dot_nt(u_ref[...], wf) + bf_ref[...]
    logf = jnp.minimum(x, 0.0) - jnp.log1p(jnp.exp(-jnp.abs(x)))
    row = lax.broadcasted_iota(jnp.int32, (tm, tm), 0)
    col = lax.broadcasted_iota(jnp.int32, (tm, tm), 1)
    if n_valid < tm:
        logf = jnp.where(lax.broadcasted_iota(jnp.int32, (tm, LANES), 0) < n_valid, logf, 0.0)
    tri = (row >= col).astype(_F32)
    c = jnp.dot(tri, logf, precision=lax.Precision.HIGHEST, preferred_element_type=_F32)
    c = c + carry_ref[...]
    c_ref[...] = c
    ct_ref[...] = c.T[:FOX_HEADS, :]
    carry_ref[...] = c[tm - 1:tm, :]


def _forget_cumsum(u, w_in_t, f_col, bf, init, nb, n_valid, tm):
    m, k = u.shape
    nt = m // (nb * tm)
    f_blk = f_col // LANES
    assert f_blk * LANES == f_col
    return pl.pallas_call(
        functools.partial(_cumsum_kernel, n_valid=n_valid),
        grid=(nb, nt),
        in_specs=[pl.BlockSpec((tm, k), lambda b, t: (b * nt + t, 0)),
                  pl.BlockSpec((LANES, k), lambda b, t: (f_blk, 0)),
                  pl.BlockSpec((1, LANES), lambda b, t: (0, 0)),
                  pl.BlockSpec((1, LANES), lambda b, t: (0, 0))],
        out_specs=[pl.BlockSpec((tm, LANES), lambda b, t: (b * nt + t, 0)),
                   pl.BlockSpec((FOX_HEADS, tm), lambda b, t: (0, b * nt + t))],
        out_shape=[jax.ShapeDtypeStruct((m, LANES), _F32),
                   jax.ShapeDtypeStruct((FOX_HEADS, m), _F32)],
        scratch_shapes=[pltpu.VMEM((1, LANES), _F32)],
        compiler_params=_params("arbitrary", "arbitrary"),
        name="forget_cumsum",
    )(u, w_in_t, bf, init)


def _dot_tn(a, b):
    return lax.dot_general(a, b, (((0,), (0,)), ((), ())), preferred_element_type=_F32)


def _lane_tile(x, n):
    return jnp.concatenate([x] * (n // LANES), axis=1)


def _flash_step(t, v, rq, m, l, acc):
    m_new = jnp.maximum(m, jnp.max(t, axis=0, keepdims=True) + rq)
    alpha = jnp.exp2(m - m_new)
    p = jnp.exp2(t - (m_new - rq))
    l = alpha * l + jnp.sum(p, axis=0, keepdims=True)
    acc = alpha * acc + _dot_tn(v, p.astype(v.dtype))
    return m_new, l, acc


def _flash_init(dv, tq):
    return (jnp.full((1, tq), _NEG_INF, _F32), jnp.zeros((1, tq), _F32), jnp.zeros((dv, tq), _F32))


def _block_start(j, size):
    return j * size if isinstance(j, int) else pl.multiple_of(j * size, size)


class _QueryTile:
    def __init__(self, streams, dv, tq, rq, first_scores, first_values, scores, values,
                 first_ref, t_ref):
        self.streams, self.dv, self.tq, self.rq = streams, dv, tq, rq
        self.first_scores, self.first_values = first_scores, first_values
        self.scores, self.values, self.first_ref, self.t_ref = scores, values, first_ref, t_ref

    def stage_first(self):
        for s in range(self.streams):
            t = self.first_scores(s)
            self.first_ref[s, :t.shape[0], :] = t
        self.first_keys = t.shape[0]

    def start(self, n_pairs):
        if n_pairs:
            for s in range(self.streams):
                self.t_ref[0, s] = self.scores(s, 0)
        state = []
        for s in range(self.streams):
            state.extend(_flash_step(self.first_ref[s, :self.first_keys, :], self.first_values(s),
                                     self.rq[s], *_flash_init(self.dv, self.tq)))
        return tuple(state)

    def run_pairs(self, n_pairs, state):
        def half(st, slot, blk, nxt):
            for s in range(self.streams):
                self.t_ref[1 - slot, s] = self.scores(s, nxt)
            out = []
            for s in range(self.streams):
                out.extend(_flash_step(self.t_ref[slot, s], self.values(s, blk), self.rq[s],
                                       *st[3 * s:3 * s + 3]))
            return tuple(out)

        def pair(j, st):
            st = half(st, 0, 2 * j, 2 * j + 1)
            return half(st, 1, 2 * j + 1, jnp.minimum(2 * j + 2, 2 * n_pairs - 1))

        return lax.fori_loop(0, n_pairs, pair, state) if n_pairs else state


def _sweep_tile_pairs(n_tiles, make_tile, finish):
    def pair_tiles(a):
        tiles = [make_tile(2 * a, 0, None, a % 2), make_tile(2 * a + 1, 1, 2 * a, a % 2)]
        for t in tiles:
            t.stage_first()
        return tiles

    tiles = pair_tiles(0)
    for a in range(n_tiles // 2):
        nxt = pair_tiles(a + 1) if a + 1 < n_tiles // 2 else None
        states = [t.start(a) for t in tiles]
        states = [t.run_pairs(a, st) for t, st in zip(tiles, states)]
        for t, st in zip(tiles, states):
            finish(t, st)
        tiles = nxt


def _fox_kernel(q_ref, k_ref, v_ref, km_ref, vm_ref, c_ref, cm_ref, ct_ref, o_ref, colb_ref,
                colbm_ref, t_ref, first_even_ref, first_odd_ref, *, tq, hps):
    nq = q_ref.shape[0] // tq
    d = HEAD_DIM
    heads = [pl.program_id(1) * hps + hh for hh in range(hps)]
    lanes = lambda hh: slice(hh * d, (hh + 1) * d)

    lane = lax.broadcasted_iota(jnp.int32, (tq, LANES), 1)
    lane_m = lax.broadcasted_iota(jnp.int32, (META_ROWS, LANES), 1)
    for hh in range(hps):
        def fill(i, carry, hh=hh):
            rs = pl.multiple_of(i * tq, tq)
            ck = jnp.sum(jnp.where(lane == heads[hh], c_ref[pl.ds(rs, tq), :], 0.0),
                         axis=1, keepdims=True)
            colb_ref[hh, pl.ds(rs, tq), :] = jnp.broadcast_to(-LOG2E * ck, (tq, LANES))
            return carry

        lax.fori_loop(0, nq, fill, 0)
        ckm = jnp.sum(jnp.where(lane_m == heads[hh], cm_ref[...], 0.0), axis=1, keepdims=True)
        colbm_ref[hh] = jnp.broadcast_to(-LOG2E * ckm, (META_ROWS, LANES))

    meta_valid = lax.broadcasted_iota(jnp.int32, (META_ROWS, tq), 0) < N_META
    causal = (lax.broadcasted_iota(jnp.int32, (tq, tq), 1)
              >= lax.broadcasted_iota(jnp.int32, (tq, tq), 0))

    def make_tile(qi, slot, odd_block, stage_slot):
        qs = qi * tq
        q = [q_ref[pl.ds(qs, tq), lanes(hh)] for hh in range(hps)]
        rq = [LOG2E * ct_ref[hh, qi] for hh in range(hps)]

        def scores(hh, j):
            ks = _block_start(j, tq)
            return (_dot_nt(k_ref[pl.ds(ks, tq), lanes(hh)], q[hh])
                    + _lane_tile(colb_ref[hh, pl.ds(ks, tq), :], tq))

        def values(hh, j):
            return v_ref[pl.ds(_block_start(j, tq), tq), lanes(hh)]

        def first_scores(hh):
            tm = _dot_nt(km_ref[:, lanes(hh)], q[hh]) + _lane_tile(colbm_ref[hh], tq)
            ts = [jnp.where(causal, scores(hh, qi), _NEG_INF), jnp.where(meta_valid, tm, _NEG_INF)]
            if odd_block is not None:
                ts.append(scores(hh, odd_block))
            return jnp.concatenate(ts, axis=0)

        def first_values(hh):
            vs = [values(hh, qi), vm_ref[:, lanes(hh)]]
            if odd_block is not None:
                vs.append(values(hh, odd_block))
            return jnp.concatenate(vs, axis=0)

        first_ref = (first_even_ref if odd_block is None else first_odd_ref).at[stage_slot]
        tile = _QueryTile(hps, d, tq, rq, first_scores, first_values, scores, values, first_ref,
                          t_ref.at[slot])
        tile.qs = qs
        return tile

    def finish(tile, state):
        for hh in range(hps):
            _, l, acc = state[3 * hh:3 * hh + 3]
            o_ref[pl.ds(tile.qs, tq), lanes(hh)] = (acc / l).T.astype(o_ref.dtype)

    _sweep_tile_pairs(nq, make_tile, finish)


def _fox_attention(proj, proj_meta, c, cm, ct, nb, seq, tq=ATT_TILE, hps=FOX_HEADS_PER_STEP):
    m = proj.shape[0]
    w = hps * HEAD_DIM
    qb, kb, vb = (i * FOX_HEADS // hps for i in range(3))
    nq = seq // tq
    tok = lambda off: pl.BlockSpec((seq, w), lambda b, s: (b, off + s))
    meta = lambda off: pl.BlockSpec((META_ROWS, w), lambda b, s: (0, off + s))
    return pl.pallas_call(
        functools.partial(_fox_kernel, tq=tq, hps=hps),
        grid=(nb, FOX_HEADS // hps),
        in_specs=[tok(qb), tok(kb), tok(vb), meta(kb), meta(vb),
                  pl.BlockSpec((seq, LANES), lambda b, s: (b, 0), pipeline_mode=pl.Buffered(1)),
                  pl.BlockSpec((META_ROWS, LANES), lambda b, s: (0, 0)),
                  pl.BlockSpec((hps, nq, 1, tq), lambda b, s: (s, b, 0, 0))],
        out_specs=pl.BlockSpec((seq, w), lambda b, s: (b, s)),
        out_shape=jax.ShapeDtypeStruct((m, FOX_HEADS * HEAD_DIM), _BF16),
        scratch_shapes=[pltpu.VMEM((hps, seq, LANES), _F32),
                        pltpu.VMEM((hps, META_ROWS, LANES), _F32),
                        pltpu.VMEM((2, 2, hps, tq, tq), _F32),
                        pltpu.VMEM((2, hps, tq + META_ROWS, tq), _F32),
                        pltpu.VMEM((2, hps, 2 * tq + META_ROWS, tq), _F32)],
        compiler_params=_params("parallel", "arbitrary"),
        name="fox_attention",
    )(proj, proj, proj, proj_meta, proj_meta, c, cm, ct)


def _diff_kernel(q_ref, k_ref, v_ref, km_ref, vm_ref, slope_ref, lq1_ref, lk1_ref, lq2_ref,
                 lk2_ref, g_ref, o_ref, colb_ref, t_ref, first_even_ref, first_odd_ref, *, tq,
                 lambda_init):
    nq = q_ref.shape[0] // tq
    d = HEAD_DIM
    comp = lambda c: slice(c * d, (c + 1) * d)
    slope2 = LOG2E * slope_ref[:, 0:1]

    key_row = lax.broadcasted_iota(jnp.int32, (tq, LANES), 0)

    def fill(i, carry):
        rs = pl.multiple_of(i * tq, tq)
        colb_ref[pl.ds(rs, tq), :] = slope2 * (key_row + (rs + N_META)).astype(_F32)
        return carry

    lax.fori_loop(0, nq, fill, 0)
    colbm = slope2 * lax.broadcasted_iota(jnp.int32, (META_ROWS, LANES), 0).astype(_F32)
    query_col = lax.broadcasted_iota(jnp.int32, (1, tq), 1)

    meta_valid = lax.broadcasted_iota(jnp.int32, (META_ROWS, tq), 0) < N_META
    causal = (lax.broadcasted_iota(jnp.int32, (tq, tq), 1)
              >= lax.broadcasted_iota(jnp.int32, (tq, tq), 0))
    lam = (jnp.exp(jnp.sum(lq1_ref[...] * lk1_ref[...], axis=1, keepdims=True))
           - jnp.exp(jnp.sum(lq2_ref[...] * lk2_ref[...], axis=1, keepdims=True)) + lambda_init)

    def make_tile(qi, slot, odd_block, stage_slot):
        qs = qi * tq
        q = [q_ref[pl.ds(qs, tq), comp(c)] for c in range(2)]
        rq = -slope2 * (query_col + (qs + N_META)).astype(_F32)

        def scores(c, j):
            ks = _block_start(j, tq)
            return (_dot_nt(k_ref[pl.ds(ks, tq), comp(c)], q[c])
                    + _lane_tile(colb_ref[pl.ds(ks, tq), :], tq))

        def values(c, j):
            return v_ref[pl.ds(_block_start(j, tq), tq), :]

        def first_scores(c):
            tm = _dot_nt(km_ref[:, comp(c)], q[c]) + _lane_tile(colbm, tq)
            ts = [jnp.where(causal, scores(c, qi), _NEG_INF), jnp.where(meta_valid, tm, _NEG_INF)]
            if odd_block is not None:
                ts.append(scores(c, odd_block))
            return jnp.concatenate(ts, axis=0)

        def first_values(c):
            vs = [values(c, qi), vm_ref[...]]
            if odd_block is not None:
                vs.append(values(c, odd_block))
            return jnp.concatenate(vs, axis=0)

        first_ref = (first_even_ref if odd_block is None else first_odd_ref).at[stage_slot]
        tile = _QueryTile(2, 2 * d, tq, (rq, rq), first_scores, first_values, scores, values,
                          first_ref, t_ref.at[slot])
        tile.qs = qs
        return tile

    def finish(tile, state):
        _, l1, a1, _, l2, a2 = state
        o = (a1 / l1 - lam * (a2 / l2)).T
        y = _rms(o, g_ref[...]) * (1.0 - lambda_init)
        o_ref[pl.ds(tile.qs, tq), :] = y.astype(o_ref.dtype)

    _sweep_tile_pairs(nq, make_tile, finish)


def _diff_attention(proj, proj_meta, slopes, lq1, lk1, lq2, lk2, subln_g, lambda_init, nb, seq,
                    tq=ATT_TILE):
    m = proj.shape[0]
    w = 2 * HEAD_DIM
    qb, kb, vb = (i * DIFF_HEADS for i in range(3))
    tok = lambda off: pl.BlockSpec((seq, w), lambda b, h: (b, off + h))
    meta = lambda off: pl.BlockSpec((META_ROWS, w), lambda b, h: (0, off + h))
    vec = lambda n: pl.BlockSpec((1, n), lambda b, h: (0, 0))
    return pl.pallas_call(
        functools.partial(_diff_kernel, tq=tq, lambda_init=lambda_init),
        grid=(nb, DIFF_HEADS),
        in_specs=[tok(qb), tok(kb), tok(vb), meta(kb), meta(vb),
                  pl.BlockSpec((None, 1, LANES), lambda b, h: (h, 0, 0)),
                  vec(HEAD_DIM), vec(HEAD_DIM), vec(HEAD_DIM), vec(HEAD_DIM), vec(w)],
        out_specs=pl.BlockSpec((seq, w), lambda b, h: (b, h)),
        out_shape=jax.ShapeDtypeStruct((m, DIFF_HEADS * w), _BF16),
        scratch_shapes=[pltpu.VMEM((seq, LANES), _F32), pltpu.VMEM((2, 2, 2, tq, tq), _F32),
                        pltpu.VMEM((2, 2, tq + META_ROWS, tq), _F32),
                        pltpu.VMEM((2, 2, 2 * tq + META_ROWS, tq), _F32)],
        compiler_params=_params("parallel", "arbitrary"),
        name="diff_attention",
    )(proj, proj, proj, proj_meta, proj_meta, slopes, lq1, lk1, lq2, lk2, subln_g)


def kernel(x, meta_tokens, ff1_pre_g, ff1_w_gate, ff1_w_up, ff1_w_down, ff1_post_g, mix_pre_g, w_in, b_forget, lambda_q1, lambda_k1, lambda_q2, lambda_k2, diff_subln_g, w_o_fox, w_o_diff, w_out, mix_post_g, ff2_pre_g, ff2_w_gate, ff2_w_up, ff2_w_down, ff2_post_g):
    nb, seq, d_model = x.shape
    depth = w_in.shape[0]
    assert depth == 1, "deeper stacks need the meta-token query path"
    h = x.reshape(nb * seq, d_model)
    hm = jnp.pad(meta_tokens.astype(x.dtype), ((0, META_ROWS - N_META), (0, 0)))
    slopes = jnp.exp2(-ALIBI_MAX_BIAS * jnp.arange(1, DIFF_HEADS + 1, dtype=_F32) / DIFF_HEADS)
    slopes = jnp.broadcast_to(slopes[:, None, None], (DIFF_HEADS, 1, LANES))
    nq = seq // ATT_TILE
    fox_w = FOX_HEADS * HEAD_DIM
    diff_w = DIFF_HEADS * 2 * HEAD_DIM
    f_col = 3 * fox_w
    vec = lambda a: a.reshape(1, -1)

    for layer in range(depth):
        lambda_init = 0.8 - 0.6 * math.exp(-0.3 * layer)

        wg, wu = ff1_w_gate[layer], ff1_w_up[layer]
        pre, post, mix_pre = vec(ff1_pre_g[layer]), vec(ff1_post_g[layer]), vec(mix_pre_g[layer])
        hid, (wd, w, w_of, w_od, w_mix) = _gateup(
            _norm(h, pre), wg, wu,
            casts=[ff1_w_down[layer], w_in[layer].T, w_o_fox[layer], w_o_diff[layer], w_out[layer]])
        y, _ = _mm(hid, wd, SUBLAYER_OUT_DTYPE, tm=512, tn=512)
        h, u = _resid_norm(h, y, post, mix_pre, 0.5)
        hid_m, _ = _gateup(_norm(hm, pre), wg, wu)
        ym, _ = _mm(hid_m, wd, SUBLAYER_OUT_DTYPE, tm=META_ROWS, tn=512)
        hm, um = _resid_norm(hm, ym, post, mix_pre, 0.5)

        d_col = f_col + FOX_HEADS
        proj_f = _proj(u, w, 0, 3 * fox_w, fox_w)
        proj_d = _proj(u, w, d_col, 3 * diff_w, diff_w)
        gates = _proj(u, w, d_col + 3 * diff_w, 2 * d_model, None)
        proj_fm = _proj(um, w, 0, 3 * fox_w, fox_w)
        proj_dm = _proj(um, w, d_col, 3 * diff_w, diff_w)

        b_f = jnp.pad(b_forget[layer].astype(_F32), (0, LANES - FOX_HEADS)).reshape(1, LANES)
        cm, _ = _forget_cumsum(um, w, f_col, b_f, jnp.zeros((1, LANES), _F32), 1, N_META, META_ROWS)
        c, ct = _forget_cumsum(u, w, f_col, b_f, cm[META_ROWS - 1:], nb, CUMSUM_TILE, CUMSUM_TILE)
        ct = ct.reshape(FOX_HEADS, nb * nq, 1, ATT_TILE)

        fox = _fox_attention(proj_f, proj_fm, c, cm, ct, nb, seq)
        diff = _diff_attention(proj_d, proj_dm, slopes, vec(lambda_q1[layer]).astype(_F32),
                               vec(lambda_k1[layer]).astype(_F32), vec(lambda_q2[layer]).astype(_F32),
                               vec(lambda_k2[layer]).astype(_F32), vec(diff_subln_g[layer]).astype(_F32),
                               lambda_init, nb, seq)
        merged = _merge(fox, diff, w_of, w_od, gates, 0, d_model)
        mix, _ = _mm(merged, w_mix, SUBLAYER_OUT_DTYPE, tm=1024, tn=512)

        pre2, post2 = vec(ff2_pre_g[layer]), vec(ff2_post_g[layer])
        h, xn = _resid_norm(h, mix, vec(mix_post_g[layer]), pre2, 1.0)
        hid, (wd2,) = _gateup(xn, ff2_w_gate[layer], ff2_w_up[layer], casts=[ff2_w_down[layer]])
        y, _ = _mm(hid, wd2, SUBLAYER_OUT_DTYPE, tm=512, tn=512)
        h = _resid(h, y, post2, 0.5)

    return h.reshape(nb, seq, d_model)
```

```python
import functools
import math

import jax
import jax.numpy as jnp
from jax import lax
from jax.experimental import pallas as pl
from jax.experimental.pallas import tpu as pltpu

N_META = 16
FOX_HEADS = 16
DIFF_HEADS = 8
HEAD_DIM = 128
RMS_EPS = 1e-6
ALIBI_MAX_BIAS = 8.0

LANES = 128
F32_SUBLANES = 8
BF16_SUBLANES = 16
META_ROWS = 128
VMEM_LIMIT = 58 * 1024 * 1024
ATT_TILE = 512
FOX_HEADS_PER_STEP = 2
LOG2E = math.log2(math.e)
CUMSUM_TILE = 256
NORM_ROWS = 512

_F32 = jnp.float32
_BF16 = jnp.bfloat16
_NEG_INF = float("-inf")
SUBLAYER_OUT_DTYPE = _BF16


def _params(*sem):
    return pltpu.CompilerParams(dimension_semantics=sem, vmem_limit_bytes=VMEM_LIMIT)


def _row_tile(m, want):
    return want if m % want == 0 else m


def _rms(x, g):
    return x * lax.rsqrt(jnp.mean(x * x, axis=-1, keepdims=True) + RMS_EPS) * g


def _sigmoid(x):
    return 0.5 * jnp.tanh(0.5 * x) + 0.5


def _dot(a, b):
    return jnp.dot(a, b, preferred_element_type=_F32)


def _dot_nt(a, b):
    return lax.dot_general(a, b, (((1,), (1,)), ((), ())), preferred_element_type=_F32)


def _norm_kernel(h_ref, g_ref, o_ref):
    o_ref[...] = _rms(h_ref[...], g_ref[...]).astype(o_ref.dtype)


def _norm(h, g, tm=NORM_ROWS):
    m, d = h.shape
    tm = _row_tile(m, tm)
    return pl.pallas_call(
        _norm_kernel,
        grid=(m // tm,),
        in_specs=[pl.BlockSpec((tm, d), lambda i: (i, 0)),
                  pl.BlockSpec((1, d), lambda i: (0, 0))],
        out_specs=pl.BlockSpec((tm, d), lambda i: (i, 0)),
        out_shape=jax.ShapeDtypeStruct((m, d), _BF16),
        compiler_params=_params("parallel"),
        name="norm",
    )(h, g)


def _resid_norm_kernel(h_ref, y_ref, gpost_ref, gpre_ref, hn_ref, xn_ref, *, alpha):
    hn = h_ref[...] + alpha * _rms(y_ref[...].astype(_F32), gpost_ref[...])
    hn_ref[...] = hn
    xn_ref[...] = _rms(hn, gpre_ref[...]).astype(xn_ref.dtype)


def _resid_norm(h, y, gpost, gpre, alpha, tm=NORM_ROWS):
    m, d = h.shape
    tm = _row_tile(m, tm)
    row = pl.BlockSpec((tm, d), lambda i: (i, 0))
    vec = pl.BlockSpec((1, d), lambda i: (0, 0))
    return pl.pallas_call(
        functools.partial(_resid_norm_kernel, alpha=alpha),
        grid=(m // tm,),
        in_specs=[row, row, vec, vec],
        out_specs=[row, row],
        out_shape=[jax.ShapeDtypeStruct((m, d), _F32), jax.ShapeDtypeStruct((m, d), _BF16)],
        compiler_params=_params("parallel"),
        name="resid_norm",
    )(h, y, gpost, gpre)


def _resid_kernel(h_ref, y_ref, gpost_ref, o_ref, *, alpha):
    o_ref[...] = h_ref[...] + alpha * _rms(y_ref[...].astype(_F32), gpost_ref[...])


def _resid(h, y, gpost, alpha, tm=NORM_ROWS):
    m, d = h.shape
    tm = _row_tile(m, tm)
    row = pl.BlockSpec((tm, d), lambda i: (i, 0))
    vec = pl.BlockSpec((1, d), lambda i: (0, 0))
    return pl.pallas_call(
        functools.partial(_resid_kernel, alpha=alpha),
        grid=(m // tm,),
        in_specs=[row, row, vec],
        out_specs=row,
        out_shape=jax.ShapeDtypeStruct((m, d), _F32),
        compiler_params=_params("parallel"),
        name="resid",
    )(h, y, gpost)


def _matmul_call(body, n_in, grid, in_specs, out_spec, out_shape, args, casts, name):
    steps = grid[0] * grid[1]
    cast_specs, cast_shapes = [], []
    for w in casts:
        rows, cols = w.shape
        slab = -(-rows // steps)
        slab += -slab % BF16_SUBLANES
        last = -(-rows // slab) - 1
        cast_specs.append(pl.BlockSpec(
            (slab, cols), lambda i, j, last=last: (jnp.minimum(i * grid[1] + j, last), 0)))
        cast_shapes.append(jax.ShapeDtypeStruct(w.shape, _BF16))

    def kernel(*refs):
        n_cast = len(casts)
        ins, src = refs[:n_in], refs[n_in:n_in + n_cast]
        o_ref, dst = refs[n_in + n_cast], refs[n_in + n_cast + 1:]
        for s, d in zip(src, dst):
            d[...] = s[...].astype(d.dtype)
        body(*ins, o_ref)

    out = pl.pallas_call(
        kernel,
        grid=grid,
        in_specs=[*in_specs, *cast_specs],
        out_specs=[out_spec, *cast_specs],
        out_shape=[out_shape, *cast_shapes],
        compiler_params=_params("arbitrary", "arbitrary"),
        name=name,
    )(*args, *casts)
    return out[0], list(out[1:])


def _mm_kernel(a_ref, w_ref, o_ref):
    o_ref[...] = _dot(a_ref[...], w_ref[...].astype(a_ref.dtype)).astype(o_ref.dtype)


def _mm(a, w, out_dtype, tm, tn, casts=()):
    m, k = a.shape
    n = w.shape[1]
    tm = _row_tile(m, tm)
    return _matmul_call(
        _mm_kernel, 2, (m // tm, n // tn),
        [pl.BlockSpec((tm, k), lambda i, j: (i, 0)), pl.BlockSpec((k, tn), lambda i, j: (0, j))],
        pl.BlockSpec((tm, tn), lambda i, j: (i, j)), jax.ShapeDtypeStruct((m, n), out_dtype),
        [a, w], list(casts), "mm")


def _gateup_kernel(x_ref, wg_ref, wu_ref, o_ref):
    x = x_ref[...]
    g = _dot(x, wg_ref[...].astype(x.dtype))
    u = _dot(x, wu_ref[...].astype(x.dtype))
    o_ref[...] = (g * _sigmoid(g) * u).astype(o_ref.dtype)


def _gateup(x, wg, wu, tm=2048, tn=256, casts=()):
    m, k = x.shape
    n = wg.shape[1]
    tm = _row_tile(m, tm)
    wspec = pl.BlockSpec((k, tn), lambda i, j: (0, j))
    return _matmul_call(
        _gateup_kernel, 3, (m // tm, n // tn),
        [pl.BlockSpec((tm, k), lambda i, j: (i, 0), pipeline_mode=pl.Buffered(1)), wspec, wspec],
        pl.BlockSpec((tm, tn), lambda i, j: (i, j)), jax.ShapeDtypeStruct((m, n), _BF16),
        [x, wg, wu], list(casts), "gateup")


def _proj_kernel(u_ref, wt_ref, o_ref, *, tn, q_cols, q_scale):
    acc = _dot_nt(u_ref[...], wt_ref[...].astype(u_ref.dtype))
    if q_cols is None:
        o_ref[...] = _sigmoid(acc).astype(o_ref.dtype)
    else:
        scale = jnp.where(pl.program_id(1) * tn < q_cols, q_scale, 1.0).astype(_F32)
        o_ref[...] = (acc * scale).astype(o_ref.dtype)


def _proj(u, w_in_t, col0, n_out, q_cols, tm=1024, tn=1024):
    m, k = u.shape
    tm = _row_tile(m, tm)
    row_align = BF16_SUBLANES if w_in_t.dtype == _BF16 else F32_SUBLANES
    assert col0 % row_align == 0
    return pl.pallas_call(
        functools.partial(_proj_kernel, tn=tn, q_cols=q_cols, q_scale=HEAD_DIM ** -0.5 * LOG2E),
        grid=(m // tm, n_out // tn),
        in_specs=[pl.BlockSpec((tm, k), lambda i, j: (i, 0)),
                  pl.BlockSpec((pl.Element(tn), pl.Element(k)),
                               lambda i, j: (pl.multiple_of(col0 + j * tn, row_align), 0))],
        out_specs=pl.BlockSpec((tm, tn), lambda i, j: (i, j)),
        out_shape=jax.ShapeDtypeStruct((m, n_out), _BF16),
        compiler_params=_params("parallel", "arbitrary"),
        name="proj_gate" if q_cols is None else "proj_qkv",
    )(u, w_in_t)


def _merge_kernel(af_ref, ad_ref, wf_ref, wd_ref, gf_ref, gd_ref, o_ref):
    yf = _dot(af_ref[...], wf_ref[...].astype(af_ref.dtype))
    yd = _dot(ad_ref[...], wd_ref[...].astype(ad_ref.dtype))
    o_ref[...] = (gf_ref[...].astype(_F32) * yf + gd_ref[...].astype(_F32) * yd).astype(o_ref.dtype)


def _merge(fox, diff, w_of, w_od, gates, gf_col, gd_col, tm=1024, tn=512):
    m, k = fox.shape
    n = w_of.shape[1]
    aspec = pl.BlockSpec((tm, k), lambda i, j: (i, 0))
    wspec = pl.BlockSpec((k, tn), lambda i, j: (0, j))
    return pl.pallas_call(
        _merge_kernel,
        grid=(m // tm, n // tn),
        in_specs=[aspec, aspec, wspec, wspec,
                  pl.BlockSpec((tm, tn), lambda i, j: (i, gf_col // tn + j)),
                  pl.BlockSpec((tm, tn), lambda i, j: (i, gd_col // tn + j))],
        out_specs=pl.BlockSpec((tm, tn), lambda i, j: (i, j)),
        out_shape=jax.ShapeDtypeStruct((m, n), _BF16),
        compiler_params=_params("parallel", "arbitrary"),
        name="merge",
    )(fox, diff, w_of, w_od, gates, gates)


def _cumsum_kernel(u_ref, wf_ref, bf_ref, init_ref, c_ref, ct_ref, carry_ref, *, n_valid):
    tm = u_ref.shape[0]

    @pl.when(pl.program_id(1) == 0)
    def _():
        carry_ref[...] = init_ref[...]

    is_head = lax.broadcasted_iota(jnp.int32, wf_ref.shape, 0) < FOX_HEADS
    wf = jnp.where(is_head, wf_ref[...], 0.0).astype(u_ref.dtype)
    x = _dot_nt(u_ref[...], wf) + bf_ref[...]
    logf = jnp.minimum(x, 0.0) - jnp.log1p(jnp.exp(-jnp.abs(x)))
    row = lax.broadcasted_iota(jnp.int32, (tm, tm), 0)
    col = lax.broadcasted_iota(jnp.int32, (tm, tm), 1)
    if n_valid < tm:
        logf = jnp.where(lax.broadcasted_iota(jnp.int32, (tm, LANES), 0) < n_valid, logf, 0.0)
    tri = (row >= col).astype(_F32)
    c = jnp.dot(tri, logf, precision=lax.Precision.HIGHEST, preferred_element_type=_F32)
    c = c + carry_ref[...]
    c_ref[...] = c
    ct_ref[...] = c.T[:FOX_HEADS, :]
    carry_ref[...] = c[tm - 1:tm, :]


def _forget_cumsum(u, w_in_t, f_col, bf, init, nb, n_valid, tm):
    m, k = u.shape
    nt = m // (nb * tm)
    f_blk = f_col // LANES
    assert f_blk * LANES == f_col
    return pl.pallas_call(
        functools.partial(_cumsum_kernel, n_valid=n_valid),
        grid=(nb, nt),
        in_specs=[pl.BlockSpec((tm, k), lambda b, t: (b * nt + t, 0)),
                  pl.BlockSpec((LANES, k), lambda b, t: (f_blk, 0)),
                  pl.BlockSpec((1, LANES), lambda b, t: (0, 0)),
                  pl.BlockSpec((1, LANES), lambda b, t: (0, 0))],
        out_specs=[pl.BlockSpec((tm, LANES), lambda b, t: (b * nt + t, 0)),
                   pl.BlockSpec((FOX_HEADS, tm), lambda b, t: (0, b * nt + t))],
        out_shape=[jax.ShapeDtypeStruct((m, LANES), _F32),
                   jax.ShapeDtypeStruct((FOX_HEADS, m), _F32)],
        scratch_shapes=[pltpu.VMEM((1, LANES), _F32)],
        compiler_params=_params("arbitrary", "arbitrary"),
        name="forget_cumsum",
    )(u, w_in_t, bf, init)


def _dot_tn(a, b):
    return lax.dot_general(a, b, (((0,), (0,)), ((), ())), preferred_element_type=_F32)


def _lane_tile(x, n):
    return jnp.concatenate([x] * (n // LANES), axis=1)


def _flash_step(t, v, rq, m, l, acc):
    m_new = jnp.maximum(m, jnp.max(t, axis=0, keepdims=True) + rq)
    alpha = jnp.exp2(m - m_new)
    p = jnp.exp2(t - (m_new - rq))
    l = alpha * l + jnp.sum(p, axis=0, keepdims=True)
    acc = alpha * acc + _dot_tn(v, p.astype(v.dtype))
    return m_new, l, acc


def _flash_init(dv, tq):
    return (jnp.full((1, tq), _NEG_INF, _F32), jnp.zeros((1, tq), _F32), jnp.zeros((dv, tq), _F32))


def _block_start(j, size):
    return j * size if isinstance(j, int) else pl.multiple_of(j * size, size)


class _QueryTile:
    def __init__(self, streams, dv, tq, rq, first_scores, first_values, scores, values,
                 first_ref, t_ref):
        self.streams, self.dv, self.tq, self.rq = streams, dv, tq, rq
        self.first_scores, self.first_values = first_scores, first_values
        self.scores, self.values, self.first_ref, self.t_ref = scores, values, first_ref, t_ref

    def stage_first(self):
        for s in range(self.streams):
            t = self.first_scores(s)
            self.first_ref[s, :t.shape[0], :] = t
        self.first_keys = t.shape[0]

    def start(self, n_pairs):
        if n_pairs:
            for s in range(self.streams):
                self.t_ref[0, s] = self.scores(s, 0)
        state = []
        for s in range(self.streams):
            state.extend(_flash_step(self.first_ref[s, :self.first_keys, :], self.first_values(s),
                                     self.rq[s], *_flash_init(self.dv, self.tq)))
        return tuple(state)

    def run_pairs(self, n_pairs, state):
        def half(st, slot, blk, nxt):
            for s in range(self.streams):
                self.t_ref[1 - slot, s] = self.scores(s, nxt)
            out = []
            for s in range(self.streams):
                out.extend(_flash_step(self.t_ref[slot, s], self.values(s, blk), self.rq[s],
                                       *st[3 * s:3 * s + 3]))
            return tuple(out)

        def pair(j, st):
            st = half(st, 0, 2 * j, 2 * j + 1)
            return half(st, 1, 2 * j + 1, jnp.minimum(2 * j + 2, 2 * n_pairs - 1))

        return lax.fori_loop(0, n_pairs, pair, state) if n_pairs else state


def _sweep_tile_pairs(n_tiles, make_tile, finish):
    def pair_tiles(a):
        tiles = [make_tile(2 * a, 0, None, a % 2), make_tile(2 * a + 1, 1, 2 * a, a % 2)]
        for t in tiles:
            t.stage_first()
        return tiles

    tiles = pair_tiles(0)
    for a in range(n_tiles // 2):
        nxt = pair_tiles(a + 1) if a + 1 < n_tiles // 2 else None
        states = [t.start(a) for t in tiles]
        states = [t.run_pairs(a, st) for t, st in zip(tiles, states)]
        for t, st in zip(tiles, states):
            finish(t, st)
        tiles = nxt


def _fox_kernel(q_ref, k_ref, v_ref, km_ref, vm_ref, c_ref, cm_ref, ct_ref, o_ref, colb_ref,
                colbm_ref, t_ref, first_even_ref, first_odd_ref, *, tq, hps):
    nq = q_ref.shape[0] // tq
    d = HEAD_DIM
    heads = [pl.program_id(1) * hps + hh for hh in range(hps)]
    lanes = lambda hh: slice(hh * d, (hh + 1) * d)

    lane = lax.broadcasted_iota(jnp.int32, (tq, LANES), 1)
    lane_m = lax.broadcasted_iota(jnp.int32, (META_ROWS, LANES), 1)
    for hh in range(hps):
        def fill(i, carry, hh=hh):
            rs = pl.multiple_of(i * tq, tq)
            ck = jnp.sum(jnp.where(lane == heads[hh], c_ref[pl.ds(rs, tq), :], 0.0),
                         axis=1, keepdims=True)
            colb_ref[hh, pl.ds(rs, tq), :] = jnp.broadcast_to(-LOG2E * ck, (tq, LANES))
            return carry

        lax.fori_loop(0, nq, fill, 0)
        ckm = jnp.sum(jnp.where(lane_m == heads[hh], cm_ref[...], 0.0), axis=1, keepdims=True)
        colbm_ref[hh] = jnp.broadcast_to(-LOG2E * ckm, (META_ROWS, LANES))

    meta_valid = lax.broadcasted_iota(jnp.int32, (META_ROWS, tq), 0) < N_META
    causal = (lax.broadcasted_iota(jnp.int32, (tq, tq), 1)
              >= lax.broadcasted_iota(jnp.int32, (tq, tq), 0))

    def make_tile(qi, slot, odd_block, stage_slot):
        qs = qi * tq
        q = [q_ref[pl.ds(qs, tq), lanes(hh)] for hh in range(hps)]
        rq = [LOG2E * ct_ref[hh, qi] for hh in range(hps)]

        def scores(hh, j):
            ks = _block_start(j, tq)
            return (_dot_nt(k_ref[pl.ds(ks, tq), lanes(hh)], q[hh])
                    + _lane_tile(colb_ref[hh, pl.ds(ks, tq), :], tq))

        def values(hh, j):
            return v_ref[pl.ds(_block_start(j, tq), tq), lanes(hh)]

        def first_scores(hh):
            tm = _dot_nt(km_ref[:, lanes(hh)], q[hh]) + _lane_tile(colbm_ref[hh], tq)
            ts = [jnp.where(causal, scores(hh, qi), _NEG_INF), jnp.where(meta_valid, tm, _NEG_INF)]
            if odd_block is not None:
                ts.append(scores(hh, odd_block))
            return jnp.concatenate(ts, axis=0)

        def first_values(hh):
            vs = [values(hh, qi), vm_ref[:, lanes(hh)]]
            if odd_block is not None:
                vs.append(values(hh, odd_block))
            return jnp.concatenate(vs, axis=0)

        first_ref = (first_even_ref if odd_block is None else first_odd_ref).at[stage_slot]
        tile = _QueryTile(hps, d, tq, rq, first_scores, first_values, scores, values, first_ref,
                          t_ref.at[slot])
        tile.qs = qs
        return tile

    def finish(tile, state):
        for hh in range(hps):
            _, l, acc = state[3 * hh:3 * hh + 3]
            o_ref[pl.ds(tile.qs, tq), lanes(hh)] = (acc / l).T.astype(o_ref.dtype)

    _sweep_tile_pairs(nq, make_tile, finish)


def _fox_attention(proj, proj_meta, c, cm, ct, nb, seq, tq=ATT_TILE, hps=FOX_HEADS_PER_STEP):
    m = proj.shape[0]
    w = hps * HEAD_DIM
    qb, kb, vb = (i * FOX_HEADS // hps for i in range(3))
    nq = seq // tq
    tok = lambda off: pl.BlockSpec((seq, w), lambda b, s: (b, off + s))
    meta = lambda off: pl.BlockSpec((META_ROWS, w), lambda b, s: (0, off + s))
    return pl.pallas_call(
        functools.partial(_fox_kernel, tq=tq, hps=hps),
        grid=(nb, FOX_HEADS // hps),
        in_specs=[tok(qb), tok(kb), tok(vb), meta(kb), meta(vb),
                  pl.BlockSpec((seq, LANES), lambda b, s: (b, 0), pipeline_mode=pl.Buffered(1)),
                  pl.BlockSpec((META_ROWS, LANES), lambda b, s: (0, 0)),
                  pl.BlockSpec((hps, nq, 1, tq), lambda b, s: (s, b, 0, 0))],
        out_specs=pl.BlockSpec((seq, w), lambda b, s: (b, s)),
        out_shape=jax.ShapeDtypeStruct((m, FOX_HEADS * HEAD_DIM), _BF16),
        scratch_shapes=[pltpu.VMEM((hps, seq, LANES), _F32),
                        pltpu.VMEM((hps, META_ROWS, LANES), _F32),
                        pltpu.VMEM((2, 2, hps, tq, tq), _F32),
                        pltpu.VMEM((2, hps, tq + META_ROWS, tq), _F32),
                        pltpu.VMEM((2, hps, 2 * tq + META_ROWS, tq), _F32)],
        compiler_params=_params("parallel", "arbitrary"),
        name="fox_attention",
    )(proj, proj, proj, proj_meta, proj_meta, c, cm, ct)


def _diff_kernel(q_ref, k_ref, v_ref, km_ref, vm_ref, slope_ref, lq1_ref, lk1_ref, lq2_ref,
                 lk2_ref, g_ref, o_ref, colb_ref, t_ref, first_even_ref, first_odd_ref, *, tq,
                 lambda_init):
    nq = q_ref.shape[0] // tq
    d = HEAD_DIM
    comp = lambda c: slice(c * d, (c + 1) * d)
    slope2 = LOG2E * slope_ref[:, 0:1]

    key_row = lax.broadcasted_iota(jnp.int32, (tq, LANES), 0)

    def fill(i, carry):
        rs = pl.multiple_of(i * tq, tq)
        colb_ref[pl.ds(rs, tq), :] = slope2 * (key_row + (rs + N_META)).astype(_F32)
        return carry

    lax.fori_loop(0, nq, fill, 0)
    colbm = slope2 * lax.broadcasted_iota(jnp.int32, (META_ROWS, LANES), 0).astype(_F32)
    query_col = lax.broadcasted_iota(jnp.int32, (1, tq), 1)

    meta_valid = lax.broadcasted_iota(jnp.int32, (META_ROWS, tq), 0) < N_META
    causal = (lax.broadcasted_iota(jnp.int32, (tq, tq), 1)
              >= lax.broadcasted_iota(jnp.int32, (tq, tq), 0))
    lam = (jnp.exp(jnp.sum(lq1_ref[...] * lk1_ref[...], axis=1, keepdims=True))
           - jnp.exp(jnp.sum(lq2_ref[...] * lk2_ref[...], axis=1, keepdims=True)) + lambda_init)

    def make_tile(qi, slot, odd_block, stage_slot):
        qs = qi * tq
        q = [q_ref[pl.ds(qs, tq), comp(c)] for c in range(2)]
        rq = -slope2 * (query_col + (qs + N_META)).astype(_F32)

        def scores(c, j):
            ks = _block_start(j, tq)
            return (_dot_nt(k_ref[pl.ds(ks, tq), comp(c)], q[c])
                    + _lane_tile(colb_ref[pl.ds(ks, tq), :], tq))

        def values(c, j):
            return v_ref[pl.ds(_block_start(j, tq), tq), :]

        def first_scores(c):
            tm = _dot_nt(km_ref[:, comp(c)], q[c]) + _lane_tile(colbm, tq)
            ts = [jnp.where(causal, scores(c, qi), _NEG_INF), jnp.where(meta_valid, tm, _NEG_INF)]
            if odd_block is not None:
                ts.append(scores(c, odd_block))
            return jnp.concatenate(ts, axis=0)

        def first_values(c):
            vs = [values(c, qi), vm_ref[...]]
            if odd_block is not None:
                vs.append(values(c, odd_block))
            return jnp.concatenate(vs, axis=0)

        first_ref = (first_even_ref if odd_block is None else first_odd_ref).at[stage_slot]
        tile = _QueryTile(2, 2 * d, tq, (rq, rq), first_scores, first_values, scores, values,
                          first_ref, t_ref.at[slot])
        tile.qs = qs
        return tile

    def finish(tile, state):
        _, l1, a1, _, l2, a2 = state
        o = (a1 / l1 - lam * (a2 / l2)).T
        y = _rms(o, g_ref[...]) * (1.0 - lambda_init)
        o_ref[pl.ds(tile.qs, tq), :] = y.astype(o_ref.dtype)

    _sweep_tile_pairs(nq, make_tile, finish)


def _diff_attention(proj, proj_meta, slopes, lq1, lk1, lq2, lk2, subln_g, lambda_init, nb, seq,
                    tq=ATT_TILE):
    m = proj.shape[0]
    w = 2 * HEAD_DIM
    qb, kb, vb = (i * DIFF_HEADS for i in range(3))
    tok = lambda off: pl.BlockSpec((seq, w), lambda b, h: (b, off + h))
    meta = lambda off: pl.BlockSpec((META_ROWS, w), lambda b, h: (0, off + h))
    vec = lambda n: pl.BlockSpec((1, n), lambda b, h: (0, 0))
    return pl.pallas_call(
        functools.partial(_diff_kernel, tq=tq, lambda_init=lambda_init),
        grid=(nb, DIFF_HEADS),
        in_specs=[tok(qb), tok(kb), tok(vb), meta(kb), meta(vb),
                  pl.BlockSpec((None, 1, LANES), lambda b, h: (h, 0, 0)),
                  vec(HEAD_DIM), vec(HEAD_DIM), vec(HEAD_DIM), vec(HEAD_DIM), vec(w)],
        out_specs=pl.BlockSpec((seq, w), lambda b, h: (b, h)),
        out_shape=jax.ShapeDtypeStruct((m, DIFF_HEADS * w), _BF16),
        scratch_shapes=[pltpu.VMEM((seq, LANES), _F32), pltpu.VMEM((2, 2, 2, tq, tq), _F32),
                        pltpu.VMEM((2, 2, tq + META_ROWS, tq), _F32),
                        pltpu.VMEM((2, 2, 2 * tq + META_ROWS, tq), _F32)],
        compiler_params=_params("parallel", "arbitrary"),
        name="diff_attention",
    )(proj, proj, proj, proj_meta, proj_meta, slopes, lq1, lk1, lq2, lk2, subln_g)


def kernel(x, meta_tokens, ff1_pre_g, ff1_w_gate, ff1_w_up, ff1_w_down, ff1_post_g, mix_pre_g, w_in, b_forget, lambda_q1, lambda_k1, lambda_q2, lambda_k2, diff_subln_g, w_o_fox, w_o_diff, w_out, mix_post_g, ff2_pre_g, ff2_w_gate, ff2_w_up, ff2_w_down, ff2_post_g):
    nb, seq, d_model = x.shape
    depth = w_in.shape[0]
    assert depth == 1, "deeper stacks need the meta-token query path"
    h = x.reshape(nb * seq, d_model)
    hm = jnp.pad(meta_tokens.astype(x.dtype), ((0, META_ROWS - N_META), (0, 0)))
    slopes = jnp.exp2(-ALIBI_MAX_BIAS * jnp.arange(1, DIFF_HEADS + 1, dtype=_F32) / DIFF_HEADS)
    slopes = jnp.broadcast_to(slopes[:, None, None], (DIFF_HEADS, 1, LANES))
    nq = seq // ATT_TILE
    fox_w = FOX_HEADS * HEAD_DIM
    diff_w = DIFF_HEADS * 2 * HEAD_DIM
    f_col = 3 * fox_w
    vec = lambda a: a.reshape(1, -1)

    for layer in range(depth):
        lambda_init = 0.8 - 0.6 * math.exp(-0.3 * layer)

        wg, wu = ff1_w_gate[layer], ff1_w_up[layer]
        pre, post, mix_pre = vec(ff1_pre_g[layer]), vec(ff1_post_g[layer]), vec(mix_pre_g[layer])
        hid, (wd, w, w_of, w_od, w_mix) = _gateup(
            _norm(h, pre), wg, wu,
            casts=[ff1_w_down[layer], w_in[layer].T, w_o_fox[layer], w_o_diff[layer], w_out[layer]])
        y, _ = _mm(hid, wd, SUBLAYER_OUT_DTYPE, tm=512, tn=512)
        h, u = _resid_norm(h, y, post, mix_pre, 0.5)
        hid_m, _ = _gateup(_norm(hm, pre), wg, wu)
        ym, _ = _mm(hid_m, wd, SUBLAYER_OUT_DTYPE, tm=META_ROWS, tn=512)
        hm, um = _resid_norm(hm, ym, post, mix_pre, 0.5)

        d_col = f_col + FOX_HEADS
        proj_f = _proj(u, w, 0, 3 * fox_w, fox_w)
        proj_d = _proj(u, w, d_col, 3 * diff_w, diff_w)
        gates = _proj(u, w, d_col + 3 * diff_w, 2 * d_model, None)
        proj_fm = _proj(um, w, 0, 3 * fox_w, fox_w)
        proj_dm = _proj(um, w, d_col, 3 * diff_w, diff_w)

        b_f = jnp.pad(b_forget[layer].astype(_F32), (0, LANES - FOX_HEADS)).reshape(1, LANES)
        cm, _ = _forget_cumsum(um, w, f_col, b_f, jnp.zeros((1, LANES), _F32), 1, N_META, META_ROWS)
        c, ct = _forget_cumsum(u, w, f_col, b_f, cm[META_ROWS - 1:], nb, CUMSUM_TILE, CUMSUM_TILE)
        ct = ct.reshape(FOX_HEADS, nb * nq, 1, ATT_TILE)

        fox = _fox_attention(proj_f, proj_fm, c, cm, ct, nb, seq)
        diff = _diff_attention(proj_d, proj_dm, slopes, vec(lambda_q1[layer]).astype(_F32),
                               vec(lambda_k1[layer]).astype(_F32), vec(lambda_q2[layer]).astype(_F32),
                               vec(lambda_k2[layer]).astype(_F32), vec(diff_subln_g[layer]).astype(_F32),
                               lambda_init, nb, seq)
        merged = _merge(fox, diff, w_of, w_od, gates, 0, d_model)
        mix, _ = _mm(merged, w_mix, SUBLAYER_OUT_DTYPE, tm=1024, tn=1024)

        pre2, post2 = vec(ff2_pre_g[layer]), vec(ff2_post_g[layer])
        h, xn = _resid_norm(h, mix, vec(mix_post_g[layer]), pre2, 1.0)
        hid, (wd2,) = _gateup(xn, ff2_w_gate[layer], ff2_w_up[layer], casts=[ff2_w_down[layer]])
        y, _ = _mm(hid, wd2, SUBLAYER_OUT_DTYPE, tm=512, tn=512)
        h = _resid(h, y, post2, 0.5)

    return h.reshape(nb, seq, d_model)
```
